```python
import math
import jax, jax.numpy as jnp
from jax import lax
import numpy as np

D_MODEL = 2048
BATCH = 8
SEQ = 4096
DEPTH = 1

MIX_WIDTH = D_MODEL
ATTN_WIDTH = MIX_WIDTH // 2
CONV_CHANNELS = MIX_WIDTH - ATTN_WIDTH
N_ATTN_HEADS = 8
V_HEAD_DIM = ATTN_WIDTH // N_ATTN_HEADS
QK_HEAD_DIM = V_HEAD_DIM // 2
IN_COLS = 3 * ATTN_WIDTH + 2 * CONV_CHANNELS
CONV_TAPS = 31
Q_BLOCK = 128
N_GROUPS = 4
EXPERTS_PER_GROUP = 8
N_EXPERTS = N_GROUPS * EXPERTS_PER_GROUP
TOP_K = 2
D_EXPERT = D_MODEL // 2
MOE_BLOCK = 128
EPS = 1e-6
NEG_INF = -1e30

kernel_name = 'hybrid_diffattn_conformer_hmoe'


def rmsnorm(x, g):
    xf = x.astype(jnp.float32)
    y = xf * lax.rsqrt(jnp.mean(xf * xf, axis=-1, keepdims=True) + EPS)
    return (y * g.astype(jnp.float32)).astype(x.dtype)


def layernorm(x, g, b):
    xf = x.astype(jnp.float32)
    xc = xf - jnp.mean(xf, axis=-1, keepdims=True)
    var = jnp.mean(xc * xc, axis=-1, keepdims=True)
    y = xc * lax.rsqrt(var + EPS) * g.astype(jnp.float32) + b.astype(jnp.float32)
    return y.astype(x.dtype)


def lambda_init(layer):
    return 0.8 - 0.6 * math.exp(-0.3 * layer)


def alibi_slopes(n_heads):
    return 2.0 ** (-8.0 * (jnp.arange(n_heads, dtype=jnp.float32) + 1.0) / n_heads)


def diff_attention(q, k, v, lam, lam0, subln_g):
    B, S, H = q.shape[0], q.shape[1], q.shape[2]
    n_blocks = S // Q_BLOCK
    scale = QK_HEAD_DIM ** -0.5
    slopes = alibi_slopes(H)
    kpos = jnp.arange(S)
    qb = q.reshape(B, n_blocks, Q_BLOCK, H, 2, QK_HEAD_DIM).transpose(1, 0, 2, 3, 4, 5)

    def one_block(args):
        q_blk, i = args
        qpos = i * Q_BLOCK + jnp.arange(Q_BLOCK)
        s = jnp.einsum('bqhcd,bkhcd->bhcqk', q_blk, k,
                       preferred_element_type=jnp.float32) * scale
        dist = (qpos[:, None] - kpos[None, :]).astype(jnp.float32)
        s = s - slopes[None, :, None, None, None] * dist
        s = jnp.where(dist >= 0.0, s, NEG_INF)
        p = jax.nn.softmax(s, axis=-1)
        w = p[:, :, 0] - lam * p[:, :, 1]
        return jnp.einsum('bhqk,bkhd->bqhd', w.astype(v.dtype), v)

    ob = lax.map(one_block, (qb, jnp.arange(n_blocks)))
    o = ob.transpose(1, 0, 2, 3, 4).reshape(B, S, H, V_HEAD_DIM)
    o = rmsnorm(o, subln_g) * (1.0 - lam0)
    return o.reshape(B, S, H * V_HEAD_DIM)


def conformer_conv(a, gate, conv_w, conv_b, ln_g, ln_b):
    u = a * jax.nn.sigmoid(gate)
    c = lax.conv_general_dilated(u, conv_w.astype(u.dtype), window_strides=(1,),
                                 padding=[(CONV_TAPS - 1, 0)],
                                 dimension_numbers=('NWC', 'WIO', 'NWC'),
                                 feature_group_count=CONV_CHANNELS)
    c = c + conv_b
    return jax.nn.silu(layernorm(c, ln_g, ln_b))


def hier_moe(xn, w_group_router, b_group_router, w_expert_router, b_expert_router,
             w_gate, w_up, w_down):
    B, S, D = xn.shape
    T = B * S
    xf = xn.reshape(T, D)
    g_logits = (xf @ w_group_router).astype(jnp.float32) + b_group_router.astype(jnp.float32)
    g_prob = jax.nn.softmax(g_logits, axis=-1)
    g_sel = jnp.argmax(g_prob, axis=-1)
    g_w = jnp.take_along_axis(g_prob, g_sel[:, None], axis=-1)[:, 0]
    e_logits = ((xf @ w_expert_router).astype(jnp.float32)
                + b_expert_router.astype(jnp.float32)).reshape(T, N_GROUPS, EXPERTS_PER_GROUP)
    e_in_group = jnp.take_along_axis(e_logits, g_sel[:, None, None], axis=1)[:, 0]
    top_val, top_loc = lax.top_k(e_in_group, TOP_K)
    e_w = jax.nn.softmax(top_val, axis=-1) * g_w[:, None]
    e_id = g_sel[:, None] * EXPERTS_PER_GROUP + top_loc

    A = T * TOP_K
    flat_e = e_id.reshape(A)
    flat_tok = jnp.repeat(jnp.arange(T, dtype=jnp.int32), TOP_K)
    flat_w = e_w.reshape(A)
    order = jnp.argsort(flat_e)
    s_e, s_tok, s_w = flat_e[order], flat_tok[order], flat_w[order]
    counts = jnp.bincount(flat_e, length=N_EXPERTS)
    padded = (counts + MOE_BLOCK - 1) // MOE_BLOCK * MOE_BLOCK
    starts = jnp.cumsum(counts) - counts
    ends_p = jnp.cumsum(padded)
    starts_p = ends_p - padded
    dest = starts_p[s_e] + jnp.arange(A) - starts[s_e]
    n_blocks = -(-(A + N_EXPERTS * (MOE_BLOCK - 1)) // MOE_BLOCK)
    P = n_blocks * MOE_BLOCK
    pad_tok = jnp.zeros((P,), jnp.int32).at[dest].set(s_tok)
    pad_w = jnp.zeros((P,), jnp.float32).at[dest].set(s_w)
    blk_e = jnp.minimum(jnp.searchsorted(ends_p, jnp.arange(n_blocks) * MOE_BLOCK, side='right'),
                        N_EXPERTS - 1)

    def expert_block(args):
        tok, w, e = args
        rows = xf[tok]
        h = jax.nn.silu(rows @ w_gate[e]) * (rows @ w_up[e])
        return (h @ w_down[e]) * w[:, None].astype(rows.dtype)

    ys = lax.map(expert_block, (pad_tok.reshape(n_blocks, MOE_BLOCK),
                                pad_w.reshape(n_blocks, MOE_BLOCK), blk_e))
    out = jnp.zeros((T, D), xf.dtype).at[pad_tok].add(ys.reshape(P, D))
    return out.reshape(B, S, D)


def setup_inputs(seed: int = 0) -> dict:
    key = jax.random.key(seed)
    ks = jax.random.split(key, 22)
    L = DEPTH

    def nrm(k, shape, scale):
        return jax.random.normal(k, shape, jnp.float32) * scale

    return {
        'x': nrm(ks[0], (BATCH, SEQ, D_MODEL), 1.0),
        'norm_mix_g': 1.0 + nrm(ks[1], (L, D_MODEL), 0.02),
        'w_in': nrm(ks[2], (L, D_MODEL, IN_COLS), D_MODEL ** -0.5),
        'lambda_q1': nrm(ks[3], (L, QK_HEAD_DIM), 0.1),
        'lambda_k1': nrm(ks[4], (L, QK_HEAD_DIM), 0.1),
        'lambda_q2': nrm(ks[5], (L, QK_HEAD_DIM), 0.1),
        'lambda_k2': nrm(ks[6], (L, QK_HEAD_DIM), 0.1),
        'subln_g': 1.0 + nrm(ks[7], (L, V_HEAD_DIM), 0.02),
        'conv_w': nrm(ks[8], (L, CONV_TAPS, 1, CONV_CHANNELS), CONV_TAPS ** -0.5),
        'conv_b': nrm(ks[9], (L, CONV_CHANNELS), 0.02),
        'conv_ln_g': 1.0 + nrm(ks[10], (L, CONV_CHANNELS), 0.02),
        'conv_ln_b': nrm(ks[11], (L, CONV_CHANNELS), 0.02),
        'w_out': nrm(ks[12], (L, MIX_WIDTH, D_MODEL), MIX_WIDTH ** -0.5),
        'norm_ffn_g': 1.0 + nrm(ks[13], (L, D_MODEL), 0.02),
        'w_group_router': nrm(ks[14], (L, D_MODEL, N_GROUPS), D_MODEL ** -0.5),
        'b_group_router': nrm(ks[15], (L, N_GROUPS), 0.01),
        'w_expert_router': nrm(ks[16], (L, D_MODEL, N_EXPERTS), D_MODEL ** -0.5),
        'b_expert_router': nrm(ks[17], (L, N_EXPERTS), 0.01),
        'w_gate': nrm(ks[18], (L, N_EXPERTS, D_MODEL, D_EXPERT), D_MODEL ** -0.5),
        'w_up': nrm(ks[19], (L, N_EXPERTS, D_MODEL, D_EXPERT), D_MODEL ** -0.5),
        'w_down': nrm(ks[20], (L, N_EXPERTS, D_EXPERT, D_MODEL), D_EXPERT ** -0.5),
        'norm_final_g': 1.0 + nrm(ks[21], (D_MODEL,), 0.02),
    }


def reference(x, norm_mix_g, w_in, lambda_q1, lambda_k1, lambda_q2, lambda_k2, subln_g,
              conv_w, conv_b, conv_ln_g, conv_ln_b, w_out, norm_ffn_g, w_group_router,
              b_group_router, w_expert_router, b_expert_router, w_gate, w_up, w_down,
              norm_final_g):
    B, S, _ = x.shape
    h = x
    for l in range(DEPTH):
        lam0 = lambda_init(l)
        lam = (jnp.exp(jnp.sum(lambda_q1[l] * lambda_k1[l]).astype(jnp.float32))
               - jnp.exp(jnp.sum(lambda_q2[l] * lambda_k2[l]).astype(jnp.float32)) + lam0)
        xn = rmsnorm(h, norm_mix_g[l])
        proj = xn @ w_in[l]
        q, k, v, ca, cg = jnp.split(proj, [ATTN_WIDTH, 2 * ATTN_WIDTH, 3 * ATTN_WIDTH,
                                           3 * ATTN_WIDTH + CONV_CHANNELS], axis=-1)
        q = q.reshape(B, S, N_ATTN_HEADS, 2, QK_HEAD_DIM)
        k = k.reshape(B, S, N_ATTN_HEADS, 2, QK_HEAD_DIM)
        v = v.reshape(B, S, N_ATTN_HEADS, V_HEAD_DIM)
        attn_out = diff_attention(q, k, v, lam, lam0, subln_g[l])
        conv_out = conformer_conv(ca, cg, conv_w[l], conv_b[l], conv_ln_g[l], conv_ln_b[l])
        h = h + jnp.concatenate([attn_out, conv_out], axis=-1) @ w_out[l]
        h = h + hier_moe(rmsnorm(h, norm_ffn_g[l]), w_group_router[l], b_group_router[l],
                         w_expert_router[l], b_expert_router[l], w_gate[l], w_up[l], w_down[l])
    return rmsnorm(h, norm_final_g)
```

```python
import functools
import math

import jax
import jax.numpy as jnp
from jax import lax
from jax.experimental import pallas as pl
from jax.experimental.pallas import tpu as pltpu

N_HEADS = 8
QK_DIM = 64
HEAD_DIM = 2 * QK_DIM
ATTN_WIDTH = N_HEADS * HEAD_DIM
CONV_TAPS = 31
CONV_HALO = 32
N_GROUPS = 4
EXPERTS_PER_GROUP = 8
N_EXPERTS = N_GROUPS * EXPERTS_PER_GROUP
TOP_K = 2
EPS = 1e-6
NEG_INF = -1e30
LANES = 128
SUBLANES = 8
ROUTER_COLS = LANES
EXPERT_COL0 = SUBLANES
V7X_VMEM_LIMIT = 56 * 1024 * 1024

F32 = jnp.float32
BF16 = jnp.bfloat16


def _cparams(sem):
    return pltpu.CompilerParams(dimension_semantics=sem, vmem_limit_bytes=V7X_VMEM_LIMIT)


def _inproj_body(x_ref, g_ref, w_ref, o_ref, xn_ref):
    @pl.when(pl.program_id(1) == 0)
    def _():
        x = x_ref[...]
        ms = jnp.mean(x * x, axis=-1, keepdims=True)
        xn_ref[...] = (x * lax.rsqrt(ms + EPS) * g_ref[...]).astype(BF16)

    o_ref[...] = jnp.dot(xn_ref[...], w_ref[...], preferred_element_type=F32).astype(o_ref.dtype)


def _inproj(x2, g, w, tm, tn):
    T, D = x2.shape
    N = w.shape[1]
    return pl.pallas_call(
        _inproj_body,
        grid=(T // tm, N // tn),
        in_specs=[pl.BlockSpec((tm, D), lambda i, j: (i, 0)),
                  pl.BlockSpec((1, D), lambda i, j: (0, 0)),
                  pl.BlockSpec((D, tn), lambda i, j: (0, j))],
        out_specs=pl.BlockSpec((tm, tn), lambda i, j: (i, j)),
        out_shape=jax.ShapeDtypeStruct((T, N), BF16),
        scratch_shapes=[pltpu.VMEM((tm, D), BF16)],
        compiler_params=_cparams(("parallel", "arbitrary")),
        name="inproj",
    )(x2, g.reshape(1, D), w)


def _attn_body(slopes_ref, lamv_ref, q_ref, k_ref, v_ref, g_ref, o_ref, m_ref, l_ref, acc_ref,
               *, tq, tk, lam0):
    h = pl.program_id(1)
    qi = pl.program_id(2)
    r = tq // tk
    slope = slopes_ref[h]
    lv = lamv_ref[...]
    lam = (jnp.exp(jnp.sum(lv[0:1] * lv[1:2], axis=1, keepdims=True))
           - jnp.exp(jnp.sum(lv[2:3] * lv[3:4], axis=1, keepdims=True)) + lam0)

    q = q_ref[...] * (QK_DIM ** -0.5)
    lane = lax.broadcasted_iota(jnp.int32, q.shape, 1)
    zero = jnp.zeros_like(q)
    qmaps = (jnp.where(lane < QK_DIM, q, zero), jnp.where(lane >= QK_DIM, q, zero))

    m_ref[...] = jnp.full(m_ref.shape, NEG_INF, F32)
    l_ref[...] = jnp.zeros(l_ref.shape, F32)
    acc_ref[...] = jnp.zeros(acc_ref.shape, F32)
    qpos = qi * tq + lax.broadcasted_iota(jnp.int32, (tq, 1), 0)

    def step(j, masked):
        start = pl.multiple_of(j * tk, tk)
        k = k_ref[pl.ds(start, tk), :]
        v = v_ref[pl.ds(start, tk), :]
        kpos = j * tk + lax.broadcasted_iota(jnp.int32, (1, tk), 1)
        bias = slope * kpos.astype(F32)
        for idx in range(2):
            s = lax.dot_general(qmaps[idx], k, (((1,), (1,)), ((), ())),
                                preferred_element_type=F32) + bias
            if masked:
                s = jnp.where(qpos >= kpos, s, NEG_INF)
            m_old = m_ref[idx]
            m_new = jnp.maximum(m_old, jnp.max(s, axis=1, keepdims=True))
            alpha = jnp.exp(m_old - m_new)
            p = jnp.exp(s - m_new)
            l_ref[idx] = alpha * l_ref[idx] + jnp.sum(p, axis=1, keepdims=True)
            acc_ref[idx] = alpha * acc_ref[idx] + jnp.dot(p.astype(BF16), v, preferred_element_type=F32)
            m_ref[idx] = m_new

    def full_step(j, carry):
        step(j, False)
        return carry

    lax.fori_loop(0, qi * r, full_step, 0)
    for d in range(r):
        step(qi * r + d, True)

    o = acc_ref[0] / l_ref[0] - lam * (acc_ref[1] / l_ref[1])
    ms = jnp.mean(o * o, axis=1, keepdims=True)
    o_ref[...] = (o * lax.rsqrt(ms + EPS) * g_ref[...] * (1.0 - lam0)).astype(o_ref.dtype)


def _attention(proj3, slopes, lamv, subln_g, lam0, tq, tk):
    B, S, _ = proj3.shape
    body = functools.partial(_attn_body, tq=tq, tk=tk, lam0=lam0)
    return pl.pallas_call(
        body,
        grid=(B, N_HEADS, S // tq),
        in_specs=[pl.BlockSpec(memory_space=pltpu.SMEM),
                  pl.BlockSpec((4, QK_DIM), lambda b, h, i: (0, 0)),
                  pl.BlockSpec((None, tq, HEAD_DIM), lambda b, h, i: (b, i, h)),
                  pl.BlockSpec((None, S, HEAD_DIM), lambda b, h, i: (b, 0, N_HEADS + h)),
                  pl.BlockSpec((None, S, HEAD_DIM), lambda b, h, i: (b, 0, 2 * N_HEADS + h)),
                  pl.BlockSpec((1, HEAD_DIM), lambda b, h, i: (0, 0))],
        out_specs=pl.BlockSpec((None, tq, HEAD_DIM), lambda b, h, i: (b, i, h)),
        out_shape=jax.ShapeDtypeStruct((B, S, ATTN_WIDTH), BF16),
        scratch_shapes=[pltpu.VMEM((2, tq, 1), F32), pltpu.VMEM((2, tq, 1), F32),
                        pltpu.VMEM((2, tq, HEAD_DIM), F32)],
        compiler_params=_cparams(("parallel", "parallel", "arbitrary")),
        name="diffattn",
    )(slopes, lamv, proj3, proj3, proj3, subln_g.reshape(1, HEAD_DIM))


def _conv_body(a_ref, g_ref, ah_ref, gh_ref, w_ref, cb_ref, lng_ref, lnb_ref, o_ref, u_ref, c_ref,
               *, ts, rb):
    i = pl.program_id(1)
    nchunk = u_ref.shape[0]
    u = a_ref[...].astype(F32) * jax.nn.sigmoid(g_ref[...].astype(F32))
    uh = ah_ref[...].astype(F32) * jax.nn.sigmoid(gh_ref[...].astype(F32))
    uh = jnp.where(i > 0, uh, 0.0)
    for c in range(nchunk):
        u_ref[c, pl.ds(0, CONV_HALO), :] = uh[:, c * LANES:(c + 1) * LANES]
        u_ref[c, pl.ds(CONV_HALO, ts), :] = u[:, c * LANES:(c + 1) * LANES]

    first = CONV_HALO - (CONV_TAPS - 1)

    def lane_chunk(c, carry):
        for r0 in range(0, ts, rb):
            acc = jnp.zeros((rb, LANES), F32)
            for k in range(CONV_TAPS):
                acc = acc + pltpu.repeat(w_ref[c, k], rb // SUBLANES, axis=0) * u_ref[c, pl.ds(r0 + first + k, rb), :]
            c_ref[c, pl.ds(r0, rb), :] = acc
        return carry

    lax.fori_loop(0, nchunk, lane_chunk, 0)

    C = nchunk * LANES
    cs = [c_ref[c] + cb_ref[:, c * LANES:(c + 1) * LANES] for c in range(nchunk)]
    mean = jnp.sum(functools.reduce(jnp.add, cs), axis=1, keepdims=True) * (1.0 / C)
    xcs = [x - mean for x in cs]
    var = jnp.sum(functools.reduce(jnp.add, [x * x for x in xcs]), axis=1, keepdims=True) * (1.0 / C)
    rstd = lax.rsqrt(var + EPS)
    for c in range(nchunk):
        sl = slice(c * LANES, (c + 1) * LANES)
        y = xcs[c] * rstd * lng_ref[:, sl] + lnb_ref[:, sl]
        o_ref[:, sl] = (y * jax.nn.sigmoid(y)).astype(o_ref.dtype)


def _conv(proj3, conv_w, conv_b, ln_g, ln_b, ts, rb):
    B, S, _ = proj3.shape
    C = conv_w.shape[-1]
    nchunk = C // LANES
    a_blk = (3 * ATTN_WIDTH) // C
    hpt = ts // CONV_HALO
    body = functools.partial(_conv_body, ts=ts, rb=rb)
    w_b = jnp.broadcast_to(conv_w.reshape(CONV_TAPS, nchunk, 1, LANES).transpose(1, 0, 2, 3),
                           (nchunk, CONV_TAPS, SUBLANES, LANES))
    halo = lambda blk: (lambda b, i: (b, jnp.maximum(i * hpt - 1, 0), blk))
    vec = pl.BlockSpec((1, C), lambda b, i: (0, 0))
    return pl.pallas_call(
        body,
        grid=(B, S // ts),
        in_specs=[pl.BlockSpec((None, ts, C), lambda b, i: (b, i, a_blk)),
                  pl.BlockSpec((None, ts, C), lambda b, i: (b, i, a_blk + 1)),
                  pl.BlockSpec((None, CONV_HALO, C), halo(a_blk)),
                  pl.BlockSpec((None, CONV_HALO, C), halo(a_blk + 1)),
                  pl.BlockSpec((nchunk, CONV_TAPS, SUBLANES, LANES), lambda b, i: (0, 0, 0, 0)),
                  vec, vec, vec],
        out_specs=pl.BlockSpec((None, ts, C), lambda b, i: (b, i, 0)),
        out_shape=jax.ShapeDtypeStruct((B, S, C), BF16),
        scratch_shapes=[pltpu.VMEM((nchunk, ts + CONV_HALO, LANES), F32), pltpu.VMEM((nchunk, ts, LANES), F32)],
        compiler_params=_cparams(("parallel", "arbitrary")),
        name="convmod",
    )(proj3, proj3, proj3, proj3, w_b, conv_b.reshape(1, C), ln_g.reshape(1, C), ln_b.reshape(1, C))


def _outproj_body(x_ref, a_ref, c_ref, wa_ref, wc_ref, g_ref, wr_ref, br_ref, h_ref, xn_ref, lg_ref):
    acc = (jnp.dot(a_ref[...], wa_ref[...], preferred_element_type=F32)
           + jnp.dot(c_ref[...], wc_ref[...], preferred_element_type=F32))
    h = x_ref[...] + acc
    h_ref[...] = h
    ms = jnp.mean(h * h, axis=-1, keepdims=True)
    xn = h * lax.rsqrt(ms + EPS) * g_ref[...]
    hi = xn.astype(BF16)
    lo = (xn - hi.astype(F32)).astype(BF16)
    xn_ref[...] = hi
    r = (jnp.dot(hi, wr_ref[...], preferred_element_type=F32)
         + jnp.dot(lo, wr_ref[...], preferred_element_type=F32))
    lg_ref[...] = r[:, :ROUTER_COLS] + r[:, ROUTER_COLS:] + br_ref[...]


def _outproj(x2, attn2, conv2, w_attn, w_conv, g, wr, br, tm):
    T, D = x2.shape
    Ka, Kc = attn2.shape[1], conv2.shape[1]
    row = lambda n: pl.BlockSpec((tm, n), lambda i: (i, 0))
    full = lambda a, b: pl.BlockSpec((a, b), lambda i: (0, 0))
    return pl.pallas_call(
        _outproj_body,
        grid=(T // tm,),
        in_specs=[row(D), row(Ka), row(Kc), full(Ka, D), full(Kc, D), full(1, D),
                  full(D, 2 * ROUTER_COLS), full(1, ROUTER_COLS)],
        out_specs=[row(D), row(D), row(ROUTER_COLS)],
        out_shape=[jax.ShapeDtypeStruct((T, D), F32), jax.ShapeDtypeStruct((T, D), BF16),
                   jax.ShapeDtypeStruct((T, ROUTER_COLS), F32)],
        compiler_params=_cparams(("parallel",)),
        name="outproj",
    )(x2, attn2, conv2, w_attn, w_conv, g.reshape(1, D), wr, br)


def _route_body(lg_ref, id_ref, gw_ref):
    lt = lg_ref[...].T
    tr = lt.shape[1]
    row = lax.broadcasted_iota(jnp.int32, (SUBLANES, tr), 0)
    gl = jnp.where(row < N_GROUPS, lt[0:SUBLANES], -jnp.inf)
    gmax = jnp.max(gl, axis=0, keepdims=True)
    g_sel = jnp.min(jnp.where(gl == gmax, row, SUBLANES), axis=0, keepdims=True)
    g_w = 1.0 / jnp.sum(jnp.exp(gl - gmax), axis=0, keepdims=True)
    e_in = lt[EXPERT_COL0:EXPERT_COL0 + EXPERTS_PER_GROUP]
    for g in range(1, N_GROUPS):
        lo = EXPERT_COL0 + g * EXPERTS_PER_GROUP
        e_in = jnp.where(g_sel == g, lt[lo:lo + EXPERTS_PER_GROUP], e_in)
    v1 = jnp.max(e_in, axis=0, keepdims=True)
    i1 = jnp.min(jnp.where(e_in == v1, row, SUBLANES), axis=0, keepdims=True)
    e_rest = jnp.where(row == i1, -jnp.inf, e_in)
    v2 = jnp.max(e_rest, axis=0, keepdims=True)
    i2 = jnp.min(jnp.where(e_rest == v2, row, SUBLANES), axis=0, keepdims=True)
    d = jnp.exp(v2 - v1)
    w1 = g_w / (1.0 + d)
    w2 = w1 * d
    base = g_sel * EXPERTS_PER_GROUP
    id_ref[...] = jnp.where(row == 0, base + i1, jnp.where(row == 1, base + i2, 0))
    gw_ref[...] = jnp.where(row == 0, w1, jnp.where(row == 1, w2, 0.0))


def _route(logits, tr):
    T = logits.shape[0]
    return pl.pallas_call(
        _route_body,
        grid=(T // tr,),
        in_specs=[pl.BlockSpec((tr, ROUTER_COLS), lambda i: (i, 0))],
        out_specs=[pl.BlockSpec((SUBLANES, tr), lambda i: (0, i)),
                   pl.BlockSpec((SUBLANES, tr), lambda i: (0, i))],
        out_shape=[jax.ShapeDtypeStruct((SUBLANES, T), jnp.int32),
                   jax.ShapeDtypeStruct((SUBLANES, T), F32)],
        compiler_params=_cparams(("parallel",)),
        name="route",
    )(logits)


def _moe_body(te_ref, nt_ref, x_ref, wg_ref, wu_ref, wd_ref, gw_ref, y_ref):
    i = pl.program_id(0)

    @pl.when(i < nt_ref[0])
    def _():
        x = x_ref[...]
        a = jnp.dot(x, wg_ref[...], preferred_element_type=F32)
        b = jnp.dot(x, wu_ref[...], preferred_element_type=F32)
        hid = (a * jax.nn.sigmoid(a) * b).astype(BF16)
        y_ref[...] = jnp.dot(hid, wd_ref[...], preferred_element_type=F32) * gw_ref[...]

    @pl.when(i >= nt_ref[0])
    def _():
        y_ref[...] = jnp.zeros(y_ref.shape, y_ref.dtype)


def _moe(tile_expert, n_used, x_sorted, w_gate, w_up, w_down, gw_sorted, tm):
    P, D = x_sorted.shape
    F = w_gate.shape[-1]
    gs = pltpu.PrefetchScalarGridSpec(
        num_scalar_prefetch=2,
        grid=(P // tm,),
        in_specs=[pl.BlockSpec((tm, D), lambda i, te, nt: (i, 0)),
                  pl.BlockSpec((None, D, F), lambda i, te, nt: (te[i], 0, 0)),
                  pl.BlockSpec((None, D, F), lambda i, te, nt: (te[i], 0, 0)),
                  pl.BlockSpec((None, F, D), lambda i, te, nt: (te[i], 0, 0)),
                  pl.BlockSpec((tm, 1), lambda i, te, nt: (i, 0))],
        out_specs=pl.BlockSpec((tm, D), lambda i, te, nt: (i, 0)),
    )
    return pl.pallas_call(
        _moe_body,
        grid_spec=gs,
        out_shape=jax.ShapeDtypeStruct((P, D), F32),
        compiler_params=_cparams(("arbitrary",)),
        name="moe",
    )(tile_expert, n_used, x_sorted, w_gate, w_up, w_down, gw_sorted)


def _combine_body(h_ref, y0_ref, y1_ref, g_ref, o_ref, *, final_norm):
    h = h_ref[...] + y0_ref[...] + y1_ref[...]
    if final_norm:
        ms = jnp.mean(h * h, axis=-1, keepdims=True)
        h = h * lax.rsqrt(ms + EPS) * g_ref[...]
    o_ref[...] = h


def _combine(h1, y0, y1, g, final_norm, tm):
    T, D = h1.shape
    row = pl.BlockSpec((tm, D), lambda i: (i, 0))
    return pl.pallas_call(
        functools.partial(_combine_body, final_norm=final_norm),
        grid=(T // tm,),
        in_specs=[row, row, row, pl.BlockSpec((1, D), lambda i: (0, 0))],
        out_specs=row,
        out_shape=jax.ShapeDtypeStruct((T, D), F32),
        compiler_params=_cparams(("parallel",)),
        name="combine",
    )(h1, y0, y1, g.reshape(1, D))


def _router_weights(w_group, b_group, w_expert, b_expert):
    D = w_group.shape[0]
    w = jnp.zeros((D, ROUTER_COLS), F32)
    w = w.at[:, :N_GROUPS].set(w_group).at[:, EXPERT_COL0:EXPERT_COL0 + N_EXPERTS].set(w_expert)
    b = jnp.zeros((1, ROUTER_COLS), F32)
    b = b.at[0, :N_GROUPS].set(b_group).at[0, EXPERT_COL0:EXPERT_COL0 + N_EXPERTS].set(b_expert)
    w_hi = w.astype(BF16)
    w_lo = (w - w_hi.astype(F32)).astype(BF16)
    return jnp.concatenate([w_hi, w_lo], axis=1), b


def _dispatch(ids, gws, tm):
    T = ids.shape[0]
    A = T * TOP_K
    flat_e = ids.reshape(A)
    onehot = (flat_e[:, None] == jnp.arange(N_EXPERTS, dtype=jnp.int32)[None, :]).astype(jnp.int32)
    incl = jnp.cumsum(onehot, axis=0)
    rank = jnp.sum((incl - onehot) * onehot, axis=1)
    counts = incl[-1]
    padded = (counts + tm - 1) // tm * tm
    ends_p = jnp.cumsum(padded)
    starts_p = ends_p - padded
    dest = starts_p[flat_e] + rank
    P = A + N_EXPERTS * tm
    n_tiles = P // tm
    tok = jnp.repeat(jnp.arange(T, dtype=jnp.int32), TOP_K)
    tok_sorted = jnp.zeros((P,), jnp.int32).at[dest].set(tok)
    gw_sorted = jnp.zeros((P,), F32).at[dest].set(gws.reshape(A))
    tile_expert = jnp.minimum(
        jnp.searchsorted(ends_p, jnp.arange(n_tiles, dtype=jnp.int32) * tm, side="right"),
        N_EXPERTS - 1).astype(jnp.int32)
    n_used = (ends_p[-1] // tm).astype(jnp.int32).reshape(1)
    return dest.reshape(T, TOP_K), tok_sorted, gw_sorted, tile_expert, n_used


def kernel(x, norm_mix_g, w_in, lambda_q1, lambda_k1, lambda_q2, lambda_k2, subln_g, conv_w, conv_b,
           conv_ln_g, conv_ln_b, w_out, norm_ffn_g, w_group_router, b_group_router, w_expert_router,
           b_expert_router, w_gate, w_up, w_down, norm_final_g):
    B, S, D = x.shape
    T = B * S
    depth = w_in.shape[0]
    tm_proj = min(1024, T)
    tm_out = min(512, T)
    tq = min(512, S)
    tk = min(256, tq)
    ts = min(512, S)
    tm_moe = 256
    slopes = 2.0 ** (-8.0 * (jnp.arange(N_HEADS, dtype=F32) + 1.0) / N_HEADS)

    h = x.reshape(T, D)
    for l in range(depth):
        lam0 = 0.8 - 0.6 * math.exp(-0.3 * l)
        lamv = jnp.stack([lambda_q1[l], lambda_k1[l], lambda_q2[l], lambda_k2[l]]).astype(F32)
        proj = _inproj(h, norm_mix_g[l], w_in[l].astype(BF16), tm_proj, 1024)
        proj3 = proj.reshape(B, S, proj.shape[1])
        attn = _attention(proj3, slopes, lamv, subln_g[l], lam0, tq, tk)
        conv = _conv(proj3, conv_w[l], conv_b[l], conv_ln_g[l], conv_ln_b[l], ts, 64)
        w_o = w_out[l].astype(BF16)
        wr, br = _router_weights(w_group_router[l], b_group_router[l], w_expert_router[l], b_expert_router[l])
        h1, xn2, logits = _outproj(h, attn.reshape(T, ATTN_WIDTH), conv.reshape(T, -1),
                                   w_o[:ATTN_WIDTH], w_o[ATTN_WIDTH:], norm_ffn_g[l], wr, br, tm_out)
        ids8, gw8 = _route(logits, min(1024, T))
        ids = ids8[:TOP_K].T
        gws = gw8[:TOP_K].T
        dest, tok_sorted, gw_sorted, tile_expert, n_used = _dispatch(ids, gws, tm_moe)
        x_sorted = jnp.take(xn2, tok_sorted, axis=0)
        y_sorted = _moe(tile_expert, n_used, x_sorted, w_gate[l].astype(BF16), w_up[l].astype(BF16),
                        w_down[l].astype(BF16), gw_sorted.reshape(-1, 1), tm_moe)
        y0 = jnp.take(y_sorted, dest[:, 0], axis=0)
        y1 = jnp.take(y_sorted, dest[:, 1], axis=0)
        h = _combine(h1, y0, y1, norm_final_g, l == depth - 1, tm_out)
    return h.reshape(B, S, D)
```

```python
import functools
import math

import jax
import jax.numpy as jnp
from jax import lax
from jax.experimental import pallas as pl
from jax.experimental.pallas import tpu as pltpu

N_HEADS = 8
QK_DIM = 64
HEAD_DIM = 2 * QK_DIM
ATTN_WIDTH = N_HEADS * HEAD_DIM
CONV_TAPS = 31
CONV_HALO = 32
N_GROUPS = 4
EXPERTS_PER_GROUP = 8
N_EXPERTS = N_GROUPS * EXPERTS_PER_GROUP
TOP_K = 2
EPS = 1e-6
NEG_INF = -1e30
LOG2E = 1.4426950408889634
LANES = 128
SUBLANES = 8
ROUTER_COLS = LANES
EXPERT_COL0 = SUBLANES
V7X_VMEM_LIMIT = 56 * 1024 * 1024

F32 = jnp.float32
BF16 = jnp.bfloat16


def _cparams(sem):
    return pltpu.CompilerParams(dimension_semantics=sem, vmem_limit_bytes=V7X_VMEM_LIMIT)


def _inproj_body(x_ref, g_ref, w_ref, o_ref, xn_ref):
    @pl.when(pl.program_id(1) == 0)
    def _():
        x = x_ref[...]
        ms = jnp.mean(x * x, axis=-1, keepdims=True)
        xn_ref[...] = (x * lax.rsqrt(ms + EPS) * g_ref[...]).astype(BF16)

    o_ref[...] = jnp.dot(xn_ref[...], w_ref[...], preferred_element_type=F32).astype(o_ref.dtype)


def _inproj(x2, g, w, tm, tn):
    T, D = x2.shape
    N = w.shape[1]
    return pl.pallas_call(
        _inproj_body,
        grid=(T // tm, N // tn),
        in_specs=[pl.BlockSpec((tm, D), lambda i, j: (i, 0)),
                  pl.BlockSpec((1, D), lambda i, j: (0, 0)),
                  pl.BlockSpec((D, tn), lambda i, j: (0, j))],
        out_specs=pl.BlockSpec((tm, tn), lambda i, j: (i, j)),
        out_shape=jax.ShapeDtypeStruct((T, N), BF16),
        scratch_shapes=[pltpu.VMEM((tm, D), BF16)],
        compiler_params=_cparams(("parallel", "arbitrary")),
        name="inproj",
    )(x2, g.reshape(1, D), w)


def _attn_body(slopes_ref, lamv_ref, q_ref, k_ref, v_ref, g_ref, o_ref, m_ref, acc_ref, sa_ref, sb_ref,
               *, tq, lam0):
    h = pl.program_id(1)
    qi = pl.program_id(2)
    tk = tq
    slope2 = slopes_ref[h] * LOG2E
    lv = lamv_ref[...]
    lam = (jnp.exp(jnp.sum(lv[0:1] * lv[1:2], axis=1, keepdims=True))
           - jnp.exp(jnp.sum(lv[2:3] * lv[3:4], axis=1, keepdims=True)) + lam0)

    q = (q_ref[...].astype(F32) * (QK_DIM ** -0.5 * LOG2E)).astype(BF16)
    lane = lax.broadcasted_iota(jnp.int32, q.shape, 1)
    zero = jnp.zeros_like(q)
    qcat = jnp.concatenate([jnp.where(lane < QK_DIM, q, zero), jnp.where(lane >= QK_DIM, q, zero)], axis=0)

    m_ref[...] = jnp.full(m_ref.shape, NEG_INF, F32)
    acc_ref[...] = jnp.zeros(acc_ref.shape, F32)
    ones = jnp.ones((tk, HEAD_DIM), BF16)

    def scores(s_ref, j):
        start = pl.multiple_of(j * tk, tk)
        kpos = j * tk + lax.broadcasted_iota(jnp.int32, (1, tk), 1)
        s_ref[...] = lax.dot_general(qcat, k_ref[pl.ds(start, tk), :], (((1,), (1,)), ((), ())),
                                     preferred_element_type=F32) + slope2 * kpos.astype(F32)

    def accumulate(s_ref, j, masked):
        start = pl.multiple_of(j * tk, tk)
        vext = jnp.concatenate([v_ref[pl.ds(start, tk), :], ones], axis=1)
        s = s_ref[...]
        if masked:
            kpos = j * tk + lax.broadcasted_iota(jnp.int32, (1, tk), 1)
            qpos = qi * tq + lax.broadcasted_iota(jnp.int32, (tq, 1), 0)
            s = jnp.where(jnp.concatenate([qpos, qpos], axis=0) >= kpos, s, NEG_INF)
        m_old = m_ref[...]
        m_new = jnp.maximum(m_old, jnp.max(s, axis=1, keepdims=True))
        alpha = pltpu.repeat(jnp.exp2(m_old - m_new), 2, axis=1)
        p = jnp.exp2(s - pltpu.repeat(m_new, tk // LANES, axis=1)).astype(BF16)
        acc_ref[...] = alpha * acc_ref[...] + jnp.dot(p, vext, preferred_element_type=F32)
        m_ref[...] = m_new

    scores(sa_ref, 0)

    def pair(i, carry):
        j = 2 * i
        scores(sb_ref, j + 1)
        accumulate(sa_ref, j, False)
        scores(sa_ref, j + 2)
        accumulate(sb_ref, j + 1, False)
        return carry

    lax.fori_loop(0, qi // 2, pair, 0)

    @pl.when(qi % 2 == 1)
    def _():
        scores(sb_ref, qi)
        accumulate(sa_ref, qi - 1, False)
        accumulate(sb_ref, qi, True)

    @pl.when(qi % 2 == 0)
    def _():
        accumulate(sa_ref, qi, True)

    a = acc_ref[...]
    o = (a[:tq, :HEAD_DIM] / a[:tq, HEAD_DIM:] - lam * (a[tq:, :HEAD_DIM] / a[tq:, HEAD_DIM:]))
    ms = jnp.mean(o * o, axis=1, keepdims=True)
    o_ref[...] = (o * lax.rsqrt(ms + EPS) * g_ref[...] * (1.0 - lam0)).astype(o_ref.dtype)


def _attention(proj3, slopes, lamv, subln_g, lam0, tq):
    B, S, _ = proj3.shape
    body = functools.partial(_attn_body, tq=tq, lam0=lam0)
    return pl.pallas_call(
        body,
        grid=(B, N_HEADS, S // tq),
        in_specs=[pl.BlockSpec(memory_space=pltpu.SMEM),
                  pl.BlockSpec((4, QK_DIM), lambda b, h, i: (0, 0)),
                  pl.BlockSpec((None, tq, HEAD_DIM), lambda b, h, i: (b, i, h)),
                  pl.BlockSpec((None, S, HEAD_DIM), lambda b, h, i: (b, 0, N_HEADS + h)),
                  pl.BlockSpec((None, S, HEAD_DIM), lambda b, h, i: (b, 0, 2 * N_HEADS + h)),
                  pl.BlockSpec((1, HEAD_DIM), lambda b, h, i: (0, 0))],
        out_specs=pl.BlockSpec((None, tq, HEAD_DIM), lambda b, h, i: (b, i, h)),
        out_shape=jax.ShapeDtypeStruct((B, S, ATTN_WIDTH), BF16),
        scratch_shapes=[pltpu.VMEM((2 * tq, LANES), F32), pltpu.VMEM((2 * tq, 2 * HEAD_DIM), F32),
                        pltpu.VMEM((2 * tq, tq), F32), pltpu.VMEM((2 * tq, tq), F32)],
        compiler_params=_cparams(("parallel", "parallel", "arbitrary")),
        name="diffattn",
    )(slopes, lamv, proj3, proj3, proj3, subln_g.reshape(1, HEAD_DIM))


def _conv_body(a_ref, g_ref, ah_ref, gh_ref, w_ref, cb_ref, lng_ref, lnb_ref, o_ref, u_ref, c_ref,
               *, ts, rb):
    i = pl.program_id(1)
    nchunk = u_ref.shape[0]
    u = a_ref[...].astype(F32) * jax.nn.sigmoid(g_ref[...].astype(F32))
    uh = ah_ref[...].astype(F32) * jax.nn.sigmoid(gh_ref[...].astype(F32))
    uh = jnp.where(i > 0, uh, 0.0)
    for c in range(nchunk):
        u_ref[c, pl.ds(0, CONV_HALO), :] = uh[:, c * LANES:(c + 1) * LANES]
        u_ref[c, pl.ds(CONV_HALO, ts), :] = u[:, c * LANES:(c + 1) * LANES]

    first = CONV_HALO - (CONV_TAPS - 1)

    def lane_chunk(c, carry):
        for r0 in range(0, ts, rb):
            acc = jnp.zeros((rb, LANES), F32)
            for k in range(CONV_TAPS):
                acc = acc + pltpu.repeat(w_ref[c, k], rb // SUBLANES, axis=0) * u_ref[c, pl.ds(r0 + first + k, rb), :]
            c_ref[c, pl.ds(r0, rb), :] = acc
        return carry

    lax.fori_loop(0, nchunk, lane_chunk, 0)

    C = nchunk * LANES
    cs = [c_ref[c] + cb_ref[:, c * LANES:(c + 1) * LANES] for c in range(nchunk)]
    mean = jnp.sum(functools.reduce(jnp.add, cs), axis=1, keepdims=True) * (1.0 / C)
    xcs = [x - mean for x in cs]
    var = jnp.sum(functools.reduce(jnp.add, [x * x for x in xcs]), axis=1, keepdims=True) * (1.0 / C)
    rstd = lax.rsqrt(var + EPS)
    for c in range(nchunk):
        sl = slice(c * LANES, (c + 1) * LANES)
        y = xcs[c] * rstd * lng_ref[:, sl] + lnb_ref[:, sl]
        o_ref[:, sl] = (y * jax.nn.sigmoid(y)).astype(o_ref.dtype)


def _conv(proj3, conv_w, conv_b, ln_g, ln_b, ts, rb):
    B, S, _ = proj3.shape
    C = conv_w.shape[-1]
    nchunk = C // LANES
    a_blk = (3 * ATTN_WIDTH) // C
    hpt = ts // CONV_HALO
    body = functools.partial(_conv_body, ts=ts, rb=rb)
    w_b = jnp.broadcast_to(conv_w.reshape(CONV_TAPS, nchunk, 1, LANES).transpose(1, 0, 2, 3),
                           (nchunk, CONV_TAPS, SUBLANES, LANES))
    halo = lambda blk: (lambda b, i: (b, jnp.maximum(i * hpt - 1, 0), blk))
    vec = pl.BlockSpec((1, C), lambda b, i: (0, 0))
    return pl.pallas_call(
        body,
        grid=(B, S // ts),
        in_specs=[pl.BlockSpec((None, ts, C), lambda b, i: (b, i, a_blk)),
                  pl.BlockSpec((None, ts, C), lambda b, i: (b, i, a_blk + 1)),
                  pl.BlockSpec((None, CONV_HALO, C), halo(a_blk)),
                  pl.BlockSpec((None, CONV_HALO, C), halo(a_blk + 1)),
                  pl.BlockSpec((nchunk, CONV_TAPS, SUBLANES, LANES), lambda b, i: (0, 0, 0, 0)),
                  vec, vec, vec],
        out_specs=pl.BlockSpec((None, ts, C), lambda b, i: (b, i, 0)),
        out_shape=jax.ShapeDtypeStruct((B, S, C), BF16),
        scratch_shapes=[pltpu.VMEM((nchunk, ts + CONV_HALO, LANES), F32), pltpu.VMEM((nchunk, ts, LANES), F32)],
        compiler_params=_cparams(("parallel", "arbitrary")),
        name="convmod",
    )(proj3, proj3, proj3, proj3, w_b, conv_b.reshape(1, C), ln_g.reshape(1, C), ln_b.reshape(1, C))


def _outproj_body(x_ref, a_ref, c_ref, wa_ref, wc_ref, g_ref, wr_ref, br_ref, h_ref, xn_ref, lg_ref):
    acc = (jnp.dot(a_ref[...], wa_ref[...], preferred_element_type=F32)
           + jnp.dot(c_ref[...], wc_ref[...], preferred_element_type=F32))
    h = x_ref[...] + acc
    h_ref[...] = h
    ms = jnp.mean(h * h, axis=-1, keepdims=True)
    xn = h * lax.rsqrt(ms + EPS) * g_ref[...]
    hi = xn.astype(BF16)
    lo = (xn - hi.astype(F32)).astype(BF16)
    xn_ref[...] = hi
    r = (jnp.dot(hi, wr_ref[...], preferred_element_type=F32)
         + jnp.dot(lo, wr_ref[...], preferred_element_type=F32))
    lg_ref[...] = r[:, :ROUTER_COLS] + r[:, ROUTER_COLS:] + br_ref[...]


def _outproj(x2, attn2, conv2, w_attn, w_conv, g, wr, br, tm):
    T, D = x2.shape
    Ka, Kc = attn2.shape[1], conv2.shape[1]
    row = lambda n: pl.BlockSpec((tm, n), lambda i: (i, 0))
    full = lambda a, b: pl.BlockSpec((a, b), lambda i: (0, 0))
    return pl.pallas_call(
        _outproj_body,
        grid=(T // tm,),
        in_specs=[row(D), row(Ka), row(Kc), full(Ka, D), full(Kc, D), full(1, D),
                  full(D, 2 * ROUTER_COLS), full(1, ROUTER_COLS)],
        out_specs=[row(D), row(D), row(ROUTER_COLS)],
        out_shape=[jax.ShapeDtypeStruct((T, D), F32), jax.ShapeDtypeStruct((T, D), BF16),
                   jax.ShapeDtypeStruct((T, ROUTER_COLS), F32)],
        compiler_params=_cparams(("parallel",)),
        name="outproj",
    )(x2, attn2, conv2, w_attn, w_conv, g.reshape(1, D), wr, br)


def _route_body(lg_ref, id_ref, gw_ref):
    lt = lg_ref[...].T
    tr = lt.shape[1]
    row = lax.broadcasted_iota(jnp.int32, (SUBLANES, tr), 0)
    gl = jnp.where(row < N_GROUPS, lt[0:SUBLANES], -jnp.inf)
    gmax = jnp.max(gl, axis=0, keepdims=True)
    g_sel = jnp.min(jnp.where(gl == gmax, row, SUBLANES), axis=0, keepdims=True)
    g_w = 1.0 / jnp.sum(jnp.exp(gl - gmax), axis=0, keepdims=True)
    e_in = lt[EXPERT_COL0:EXPERT_COL0 + EXPERTS_PER_GROUP]
    for g in range(1, N_GROUPS):
        lo = EXPERT_COL0 + g * EXPERTS_PER_GROUP
        e_in = jnp.where(g_sel == g, lt[lo:lo + EXPERTS_PER_GROUP], e_in)
    v1 = jnp.max(e_in, axis=0, keepdims=True)
    i1 = jnp.min(jnp.where(e_in == v1, row, SUBLANES), axis=0, keepdims=True)
    e_rest = jnp.where(row == i1, -jnp.inf, e_in)
    v2 = jnp.max(e_rest, axis=0, keepdims=True)
    i2 = jnp.min(jnp.where(e_rest == v2, row, SUBLANES), axis=0, keepdims=True)
    d = jnp.exp(v2 - v1)
    w1 = g_w / (1.0 + d)
    w2 = w1 * d
    base = g_sel * EXPERTS_PER_GROUP
    id_ref[...] = jnp.where(row == 0, base + i1, jnp.where(row == 1, base + i2, 0))
    gw_ref[...] = jnp.where(row == 0, w1, jnp.where(row == 1, w2, 0.0))


def _route(logits, tr):
    T = logits.shape[0]
    return pl.pallas_call(
        _route_body,
        grid=(T // tr,),
        in_specs=[pl.BlockSpec((tr, ROUTER_COLS), lambda i: (i, 0))],
        out_specs=[pl.BlockSpec((SUBLANES, tr), lambda i: (0, i)),
                   pl.BlockSpec((SUBLANES, tr), lambda i: (0, i))],
        out_shape=[jax.ShapeDtypeStruct((SUBLANES, T), jnp.int32),
                   jax.ShapeDtypeStruct((SUBLANES, T), F32)],
        compiler_params=_cparams(("parallel",)),
        name="route",
    )(logits)


def _moe_body(te_ref, nt_ref, x_ref, wg_ref, wu_ref, wd_ref, gw_ref, y_ref):
    i = pl.program_id(0)

    @pl.when(i < nt_ref[0])
    def _():
        x = x_ref[...]
        a = jnp.dot(x, wg_ref[...], preferred_element_type=F32)
        b = jnp.dot(x, wu_ref[...], preferred_element_type=F32)
        hid = (a * jax.nn.sigmoid(a) * b).astype(BF16)
        y_ref[...] = jnp.dot(hid, wd_ref[...], preferred_element_type=F32) * gw_ref[...]

    @pl.when(i >= nt_ref[0])
    def _():
        y_ref[...] = jnp.zeros(y_ref.shape, y_ref.dtype)


def _moe(tile_expert, n_used, x_sorted, w_gate, w_up, w_down, gw_sorted, tm):
    P, D = x_sorted.shape
    F = w_gate.shape[-1]
    gs = pltpu.PrefetchScalarGridSpec(
        num_scalar_prefetch=2,
        grid=(P // tm,),
        in_specs=[pl.BlockSpec((tm, D), lambda i, te, nt: (i, 0)),
                  pl.BlockSpec((None, D, F), lambda i, te, nt: (te[i], 0, 0)),
                  pl.BlockSpec((None, D, F), lambda i, te, nt: (te[i], 0, 0)),
                  pl.BlockSpec((None, F, D), lambda i, te, nt: (te[i], 0, 0)),
                  pl.BlockSpec((tm, 1), lambda i, te, nt: (i, 0))],
        out_specs=pl.BlockSpec((tm, D), lambda i, te, nt: (i, 0)),
    )
    return pl.pallas_call(
        _moe_body,
        grid_spec=gs,
        out_shape=jax.ShapeDtypeStruct((P, D), F32),
        compiler_params=_cparams(("arbitrary",)),
        name="moe",
    )(tile_expert, n_used, x_sorted, w_gate, w_up, w_down, gw_sorted)


def _combine_body(h_ref, y0_ref, y1_ref, g_ref, o_ref, *, final_norm):
    h = h_ref[...] + y0_ref[...] + y1_ref[...]
    if final_norm:
        ms = jnp.mean(h * h, axis=-1, keepdims=True)
        h = h * lax.rsqrt(ms + EPS) * g_ref[...]
    o_ref[...] = h


def _combine(h1, y0, y1, g, final_norm, tm):
    T, D = h1.shape
    row = pl.BlockSpec((tm, D), lambda i: (i, 0))
    return pl.pallas_call(
        functools.partial(_combine_body, final_norm=final_norm),
        grid=(T // tm,),
        in_specs=[row, row, row, pl.BlockSpec((1, D), lambda i: (0, 0))],
        out_specs=row,
        out_shape=jax.ShapeDtypeStruct((T, D), F32),
        compiler_params=_cparams(("parallel",)),
        name="combine",
    )(h1, y0, y1, g.reshape(1, D))


def _router_weights(w_group, b_group, w_expert, b_expert):
    D = w_group.shape[0]
    w = jnp.zeros((D, ROUTER_COLS), F32)
    w = w.at[:, :N_GROUPS].set(w_group).at[:, EXPERT_COL0:EXPERT_COL0 + N_EXPERTS].set(w_expert)
    b = jnp.zeros((1, ROUTER_COLS), F32)
    b = b.at[0, :N_GROUPS].set(b_group).at[0, EXPERT_COL0:EXPERT_COL0 + N_EXPERTS].set(b_expert)
    w_hi = w.astype(BF16)
    w_lo = (w - w_hi.astype(F32)).astype(BF16)
    return jnp.concatenate([w_hi, w_lo], axis=1), b


def _dispatch(ids, gws, tm):
    T = ids.shape[0]
    A = T * TOP_K
    flat_e = ids.reshape(A)
    onehot = (flat_e[:, None] == jnp.arange(N_EXPERTS, dtype=jnp.int32)[None, :]).astype(jnp.int32)
    incl = jnp.cumsum(onehot, axis=0)
    rank = jnp.sum((incl - onehot) * onehot, axis=1)
    counts = incl[-1]
    padded = (counts + tm - 1) // tm * tm
    ends_p = jnp.cumsum(padded)
    starts_p = ends_p - padded
    dest = starts_p[flat_e] + rank
    P = A + N_EXPERTS * tm
    n_tiles = P // tm
    tok = jnp.repeat(jnp.arange(T, dtype=jnp.int32), TOP_K)
    tok_sorted = jnp.zeros((P,), jnp.int32).at[dest].set(tok)
    gw_sorted = jnp.zeros((P,), F32).at[dest].set(gws.reshape(A))
    tile_expert = jnp.minimum(
        jnp.searchsorted(ends_p, jnp.arange(n_tiles, dtype=jnp.int32) * tm, side="right"),
        N_EXPERTS - 1).astype(jnp.int32)
    n_used = (ends_p[-1] // tm).astype(jnp.int32).reshape(1)
    return dest.reshape(T, TOP_K), tok_sorted, gw_sorted, tile_expert, n_used


def kernel(x, norm_mix_g, w_in, lambda_q1, lambda_k1, lambda_q2, lambda_k2, subln_g, conv_w, conv_b,
           conv_ln_g, conv_ln_b, w_out, norm_ffn_g, w_group_router, b_group_router, w_expert_router,
           b_expert_router, w_gate, w_up, w_down, norm_final_g):
    B, S, D = x.shape
    T = B * S
    depth = w_in.shape[0]
    tm_proj = min(1024, T)
    tm_out = min(512, T)
    tq = min(512, S)
    ts = min(512, S)
    tm_moe = 256
    slopes = 2.0 ** (-8.0 * (jnp.arange(N_HEADS, dtype=F32) + 1.0) / N_HEADS)

    h = x.reshape(T, D)
    for l in range(depth):
        lam0 = 0.8 - 0.6 * math.exp(-0.3 * l)
        lamv = jnp.stack([lambda_q1[l], lambda_k1[l], lambda_q2[l], lambda_k2[l]]).astype(F32)
        proj = _inproj(h, norm_mix_g[l], w_in[l].astype(BF16), tm_proj, 1024)
        proj3 = proj.reshape(B, S, proj.shape[1])
        attn = _attention(proj3, slopes, lamv, subln_g[l], lam0, tq)
        conv = _conv(proj3, conv_w[l], conv_b[l], conv_ln_g[l], conv_ln_b[l], ts, 64)
        w_o = w_out[l].astype(BF16)
        wr, br = _router_weights(w_group_router[l], b_group_router[l], w_expert_router[l], b_expert_router[l])
        h1, xn2, logits = _outproj(h, attn.reshape(T, ATTN_WIDTH), conv.reshape(T, -1),
                                   w_o[:ATTN_WIDTH], w_o[ATTN_WIDTH:], norm_ffn_g[l], wr, br, tm_out)
        ids8, gw8 = _route(logits, min(1024, T))
        ids = ids8[:TOP_K].T
        gws = gw8[:TOP_K].T
        dest, tok_sorted, gw_sorted, tile_expert, n_used = _dispatch(ids, gws, tm_moe)
        x_sorted = jnp.take(xn2, tok_sorted, axis=0)
        y_sorted = _moe(tile_expert, n_used, x_sorted, w_gate[l].astype(BF16), w_up[l].astype(BF16),
                        w_down[l].astype(BF16), gw_sorted.reshape(-1, 1), tm_moe)
        y0 = jnp.take(y_sorted, dest[:, 0], axis=0)
        y1 = jnp.take(y_sorted, dest[:, 1], axis=0)
        h = _combine(h1, y0, y1, norm_final_g, l == depth - 1, tm_out)
    return h.reshape(B, S, D)
```

```python
import functools
import math

import jax
import jax.numpy as jnp
from jax import lax
from jax.experimental import pallas as pl
from jax.experimental.pallas import tpu as pltpu

N_HEADS = 8
QK_DIM = 64
HEAD_DIM = 2 * QK_DIM
ATTN_WIDTH = N_HEADS * HEAD_DIM
CONV_TAPS = 31
CONV_HALO = 32
N_GROUPS = 4
EXPERTS_PER_GROUP = 8
N_EXPERTS = N_GROUPS * EXPERTS_PER_GROUP
TOP_K = 2
EPS = 1e-6
NEG_INF = -1e30
LOG2E = 1.4426950408889634
LANES = 128
SUBLANES = 8
ROUTER_COLS = LANES
EXPERT_COL0 = SUBLANES
W_UNITS = 8
V7X_VMEM_LIMIT = 56 * 1024 * 1024

F32 = jnp.float32
BF16 = jnp.bfloat16


def _cparams(sem):
    return pltpu.CompilerParams(dimension_semantics=sem, vmem_limit_bytes=V7X_VMEM_LIMIT)


def _inproj_body(x_ref, g_ref, w_ref, o_ref, xn_ref):
    @pl.when(pl.program_id(1) == 0)
    def _():
        x = x_ref[...]
        ms = jnp.mean(x * x, axis=-1, keepdims=True)
        xn_ref[...] = (x * lax.rsqrt(ms + EPS) * g_ref[...]).astype(BF16)

    o_ref[...] = jnp.dot(xn_ref[...], w_ref[...], preferred_element_type=F32).astype(o_ref.dtype)


def _inproj(x2, g, w, tm, tn):
    T, D = x2.shape
    N = w.shape[1]
    return pl.pallas_call(
        _inproj_body,
        grid=(T // tm, N // tn),
        in_specs=[pl.BlockSpec((tm, D), lambda i, j: (i, 0)),
                  pl.BlockSpec((1, D), lambda i, j: (0, 0)),
                  pl.BlockSpec((D, tn), lambda i, j: (0, j))],
        out_specs=pl.BlockSpec((tm, tn), lambda i, j: (i, j)),
        out_shape=jax.ShapeDtypeStruct((T, N), BF16),
        scratch_shapes=[pltpu.VMEM((tm, D), BF16)],
        compiler_params=_cparams(("parallel", "arbitrary")),
        name="inproj",
    )(x2, g.reshape(1, D), w)


def _attn_body(slopes_ref, lamv_ref, q_ref, k_ref, v_ref, g_ref, o_ref, m_ref, acc_ref, sa_ref, sb_ref,
               *, tq, lam0):
    h = pl.program_id(1)
    qi = pl.program_id(2)
    tk = tq
    slope2 = slopes_ref[h] * LOG2E
    lv = lamv_ref[...]
    lam = (jnp.exp(jnp.sum(lv[0:1] * lv[1:2], axis=1, keepdims=True))
           - jnp.exp(jnp.sum(lv[2:3] * lv[3:4], axis=1, keepdims=True)) + lam0)

    q = (q_ref[...].astype(F32) * (QK_DIM ** -0.5 * LOG2E)).astype(BF16)
    lane = lax.broadcasted_iota(jnp.int32, q.shape, 1)
    zero = jnp.zeros_like(q)
    qcat = jnp.concatenate([jnp.where(lane < QK_DIM, q, zero), jnp.where(lane >= QK_DIM, q, zero)], axis=0)

    m_ref[...] = jnp.full(m_ref.shape, NEG_INF, F32)
    acc_ref[...] = jnp.zeros(acc_ref.shape, F32)
    ones = jnp.ones((tk, HEAD_DIM), BF16)

    def scores(s_ref, j):
        start = pl.multiple_of(j * tk, tk)
        kpos = j * tk + lax.broadcasted_iota(jnp.int32, (1, tk), 1)
        s_ref[...] = lax.dot_general(qcat, k_ref[pl.ds(start, tk), :], (((1,), (1,)), ((), ())),
                                     preferred_element_type=F32) + slope2 * kpos.astype(F32)

    def accumulate(s_ref, j, masked):
        start = pl.multiple_of(j * tk, tk)
        vext = jnp.concatenate([v_ref[pl.ds(start, tk), :], ones], axis=1)
        s = s_ref[...]
        if masked:
            kpos = j * tk + lax.broadcasted_iota(jnp.int32, (1, tk), 1)
            qpos = qi * tq + lax.broadcasted_iota(jnp.int32, (tq, 1), 0)
            s = jnp.where(jnp.concatenate([qpos, qpos], axis=0) >= kpos, s, NEG_INF)
        m_old = m_ref[...]
        m_new = jnp.maximum(m_old, jnp.max(s, axis=1, keepdims=True))
        alpha = jnp.tile(jnp.exp2(m_old - m_new), (1, 2))
        p = jnp.exp2(s - jnp.tile(m_new, (1, tk // LANES))).astype(BF16)
        acc_ref[...] = alpha * acc_ref[...] + jnp.dot(p, vext, preferred_element_type=F32)
        m_ref[...] = m_new

    scores(sa_ref, 0)

    def pair(i, carry):
        j = 2 * i
        scores(sb_ref, j + 1)
        accumulate(sa_ref, j, False)
        scores(sa_ref, j + 2)
        accumulate(sb_ref, j + 1, False)
        return carry

    lax.fori_loop(0, qi // 2, pair, 0)

    @pl.when(qi % 2 == 1)
    def _():
        scores(sb_ref, qi)
        accumulate(sa_ref, qi - 1, False)
        accumulate(sb_ref, qi, True)

    @pl.when(qi % 2 == 0)
    def _():
        accumulate(sa_ref, qi, True)

    a = acc_ref[...]
    o = (a[:tq, :HEAD_DIM] / a[:tq, HEAD_DIM:] - lam * (a[tq:, :HEAD_DIM] / a[tq:, HEAD_DIM:]))
    ms = jnp.mean(o * o, axis=1, keepdims=True)
    o_ref[...] = (o * lax.rsqrt(ms + EPS) * g_ref[...] * (1.0 - lam0)).astype(o_ref.dtype)


def _attention(proj3, slopes, lamv, subln_g, lam0, tq):
    B, S, _ = proj3.shape
    body = functools.partial(_attn_body, tq=tq, lam0=lam0)
    return pl.pallas_call(
        body,
        grid=(B, N_HEADS, S // tq),
        in_specs=[pl.BlockSpec(memory_space=pltpu.SMEM),
                  pl.BlockSpec((4, QK_DIM), lambda b, h, i: (0, 0)),
                  pl.BlockSpec((None, tq, HEAD_DIM), lambda b, h, i: (b, i, h)),
                  pl.BlockSpec((None, S, HEAD_DIM), lambda b, h, i: (b, 0, N_HEADS + h)),
                  pl.BlockSpec((None, S, HEAD_DIM), lambda b, h, i: (b, 0, 2 * N_HEADS + h)),
                  pl.BlockSpec((1, HEAD_DIM), lambda b, h, i: (0, 0))],
        out_specs=pl.BlockSpec((None, tq, HEAD_DIM), lambda b, h, i: (b, i, h)),
        out_shape=jax.ShapeDtypeStruct((B, S, ATTN_WIDTH), BF16),
        scratch_shapes=[pltpu.VMEM((2 * tq, LANES), F32), pltpu.VMEM((2 * tq, 2 * HEAD_DIM), F32),
                        pltpu.VMEM((2 * tq, tq), F32), pltpu.VMEM((2 * tq, tq), F32)],
        compiler_params=_cparams(("parallel", "parallel", "arbitrary")),
        name="diffattn",
    )(slopes, lamv, proj3, proj3, proj3, subln_g.reshape(1, HEAD_DIM))


def _conv_body(a_ref, g_ref, ah_ref, gh_ref, w_ref, cb_ref, lng_ref, lnb_ref, o_ref, u_ref, c_ref,
               *, ts, rb):
    i = pl.program_id(1)
    nchunk = u_ref.shape[0]
    u = a_ref[...].astype(F32) * jax.nn.sigmoid(g_ref[...].astype(F32))
    uh = ah_ref[...].astype(F32) * jax.nn.sigmoid(gh_ref[...].astype(F32))
    uh = jnp.where(i > 0, uh, 0.0)
    for c in range(nchunk):
        u_ref[c, pl.ds(0, CONV_HALO), :] = uh[:, c * LANES:(c + 1) * LANES]
        u_ref[c, pl.ds(CONV_HALO, ts), :] = u[:, c * LANES:(c + 1) * LANES]

    first = CONV_HALO - (CONV_TAPS - 1)

    def lane_chunk(c, carry):
        for r0 in range(0, ts, rb):
            acc = jnp.zeros((rb, LANES), F32)
            for k in range(CONV_TAPS):
                acc = acc + jnp.tile(w_ref[c, k], (rb // SUBLANES, 1)) * u_ref[c, pl.ds(r0 + first + k, rb), :]
            c_ref[c, pl.ds(r0, rb), :] = acc
        return carry

    lax.fori_loop(0, nchunk, lane_chunk, 0)

    C = nchunk * LANES
    cs = [c_ref[c] + cb_ref[:, c * LANES:(c + 1) * LANES] for c in range(nchunk)]
    mean = jnp.sum(functools.reduce(jnp.add, cs), axis=1, keepdims=True) * (1.0 / C)
    xcs = [x - mean for x in cs]
    var = jnp.sum(functools.reduce(jnp.add, [x * x for x in xcs]), axis=1, keepdims=True) * (1.0 / C)
    rstd = lax.rsqrt(var + EPS)
    for c in range(nchunk):
        sl = slice(c * LANES, (c + 1) * LANES)
        y = xcs[c] * rstd * lng_ref[:, sl] + lnb_ref[:, sl]
        o_ref[:, sl] = (y * jax.nn.sigmoid(y)).astype(o_ref.dtype)


def _conv(proj3, conv_w, conv_b, ln_g, ln_b, ts, rb):
    B, S, _ = proj3.shape
    C = conv_w.shape[-1]
    nchunk = C // LANES
    a_blk = (3 * ATTN_WIDTH) // C
    hpt = ts // CONV_HALO
    body = functools.partial(_conv_body, ts=ts, rb=rb)
    w_b = jnp.broadcast_to(conv_w.reshape(CONV_TAPS, nchunk, 1, LANES).transpose(1, 0, 2, 3),
                           (nchunk, CONV_TAPS, SUBLANES, LANES))
    halo = lambda blk: (lambda b, i: (b, jnp.maximum(i * hpt - 1, 0), blk))
    vec = pl.BlockSpec((1, C), lambda b, i: (0, 0))
    return pl.pallas_call(
        body,
        grid=(B, S // ts),
        in_specs=[pl.BlockSpec((None, ts, C), lambda b, i: (b, i, a_blk)),
                  pl.BlockSpec((None, ts, C), lambda b, i: (b, i, a_blk + 1)),
                  pl.BlockSpec((None, CONV_HALO, C), halo(a_blk)),
                  pl.BlockSpec((None, CONV_HALO, C), halo(a_blk + 1)),
                  pl.BlockSpec((nchunk, CONV_TAPS, SUBLANES, LANES), lambda b, i: (0, 0, 0, 0)),
                  vec, vec, vec],
        out_specs=pl.BlockSpec((None, ts, C), lambda b, i: (b, i, 0)),
        out_shape=jax.ShapeDtypeStruct((B, S, C), BF16),
        scratch_shapes=[pltpu.VMEM((nchunk, ts + CONV_HALO, LANES), F32), pltpu.VMEM((nchunk, ts, LANES), F32)],
        compiler_params=_cparams(("parallel", "arbitrary")),
        name="convmod",
    )(proj3, proj3, proj3, proj3, w_b, conv_b.reshape(1, C), ln_g.reshape(1, C), ln_b.reshape(1, C))


def _rmsnorm_rows(h, g):
    ms = jnp.mean(h * h, axis=-1, keepdims=True)
    return h * lax.rsqrt(ms + EPS) * g


def _outproj_body(x_ref, a_ref, c_ref, wa_ref, wc_ref, g_ref, wr_ref, br_ref, h_ref, lg_ref):
    acc = (jnp.dot(a_ref[...], wa_ref[...], preferred_element_type=F32)
           + jnp.dot(c_ref[...], wc_ref[...], preferred_element_type=F32))
    h = x_ref[...] + acc
    h_ref[...] = h
    xn = _rmsnorm_rows(h, g_ref[...])
    hi = xn.astype(BF16)
    lo = (xn - hi.astype(F32)).astype(BF16)
    r = (jnp.dot(hi, wr_ref[...], preferred_element_type=F32)
         + jnp.dot(lo, wr_ref[...], preferred_element_type=F32))
    lg_ref[...] = r[:, :ROUTER_COLS] + r[:, ROUTER_COLS:] + br_ref[...]


def _outproj(x2, attn2, conv2, w_attn, w_conv, g, wr, br, tm):
    T, D = x2.shape
    Ka, Kc = attn2.shape[1], conv2.shape[1]
    row = lambda n: pl.BlockSpec((tm, n), lambda i: (i, 0))
    full = lambda a, b: pl.BlockSpec((a, b), lambda i: (0, 0))
    return pl.pallas_call(
        _outproj_body,
        grid=(T // tm,),
        in_specs=[row(D), row(Ka), row(Kc), full(Ka, D), full(Kc, D), full(1, D),
                  full(D, 2 * ROUTER_COLS), full(1, ROUTER_COLS)],
        out_specs=[row(D), row(ROUTER_COLS)],
        out_shape=[jax.ShapeDtypeStruct((T, D), F32), jax.ShapeDtypeStruct((T, ROUTER_COLS), F32)],
        compiler_params=_cparams(("parallel",)),
        name="outproj",
    )(x2, attn2, conv2, w_attn, w_conv, g.reshape(1, D), wr, br)


def _route_body(lg_ref, id_ref, gw_ref, cnt_ref):
    lt = lg_ref[...].T
    tr = lt.shape[1]
    row = lax.broadcasted_iota(jnp.int32, (SUBLANES, tr), 0)
    gl = jnp.where(row < N_GROUPS, lt[0:SUBLANES], -jnp.inf)
    gmax = jnp.max(gl, axis=0, keepdims=True)
    g_sel = jnp.min(jnp.where(gl == gmax, row, SUBLANES), axis=0, keepdims=True)
    g_w = 1.0 / jnp.sum(jnp.exp(gl - gmax), axis=0, keepdims=True)
    e_in = lt[EXPERT_COL0:EXPERT_COL0 + EXPERTS_PER_GROUP]
    for g in range(1, N_GROUPS):
        lo = EXPERT_COL0 + g * EXPERTS_PER_GROUP
        e_in = jnp.where(g_sel == g, lt[lo:lo + EXPERTS_PER_GROUP], e_in)
    v1 = jnp.max(e_in, axis=0, keepdims=True)
    i1 = jnp.min(jnp.where(e_in == v1, row, SUBLANES), axis=0, keepdims=True)
    e_rest = jnp.where(row == i1, -jnp.inf, e_in)
    v2 = jnp.max(e_rest, axis=0, keepdims=True)
    i2 = jnp.min(jnp.where(e_rest == v2, row, SUBLANES), axis=0, keepdims=True)
    d = jnp.exp(v2 - v1)
    w1 = g_w / (1.0 + d)
    w2 = w1 * d
    e1 = g_sel * EXPERTS_PER_GROUP + i1
    e2 = g_sel * EXPERTS_PER_GROUP + i2
    id_ref[...] = jnp.where(row == 0, e1, jnp.where(row == 1, e2, 0))
    rowl = lax.broadcasted_iota(jnp.int32, (LANES, tr), 0)
    gw_ref[...] = jnp.where(rowl == 0, w1, jnp.where(rowl == 1, w2, 0.0)).T
    rowe = lax.broadcasted_iota(jnp.int32, (N_EXPERTS, tr), 0)
    hits = jnp.where(rowe == e1, 1.0, 0.0) + jnp.where(rowe == e2, 1.0, 0.0)

    @pl.when(pl.program_id(0) == 0)
    def _():
        cnt_ref[...] = jnp.zeros(cnt_ref.shape, F32)

    cnt_ref[...] += jnp.broadcast_to(jnp.sum(hits, axis=1, keepdims=True), cnt_ref.shape)


def _route(logits, tr):
    T = logits.shape[0]
    return pl.pallas_call(
        _route_body,
        grid=(T // tr,),
        in_specs=[pl.BlockSpec((tr, ROUTER_COLS), lambda i: (i, 0))],
        out_specs=[pl.BlockSpec((SUBLANES, tr), lambda i: (0, i)),
                   pl.BlockSpec((tr, LANES), lambda i: (i, 0)),
                   pl.BlockSpec((N_EXPERTS, LANES), lambda i: (0, 0))],
        out_shape=[jax.ShapeDtypeStruct((SUBLANES, T), jnp.int32),
                   jax.ShapeDtypeStruct((T, LANES), F32),
                   jax.ShapeDtypeStruct((N_EXPERTS, LANES), F32)],
        compiler_params=_cparams(("arbitrary",)),
        name="route",
    )(logits)


def _plan_body(id_ref, st_ref, u_ref, d_ref, base_ref):
    @pl.when(pl.program_id(0) == 0)
    def _():
        base_ref[...] = st_ref[...]

    ids = id_ref[...]
    tr = ids.shape[1]
    rowe = lax.broadcasted_iota(jnp.int32, (N_EXPERTS, tr), 0)
    base = base_ref[:, 0:1]
    dests = []
    for k in range(TOP_K):
        hit = rowe == ids[k:k + 1]
        before = jnp.dot(jnp.where(hit, 1.0, 0.0).astype(BF16), u_ref[...], preferred_element_type=F32)
        dests.append(jnp.sum(jnp.where(hit, base + before, 0.0), axis=0, keepdims=True))
        base = base + jnp.sum(jnp.where(hit, 1.0, 0.0), axis=1, keepdims=True)
    base_ref[...] = jnp.broadcast_to(base, base_ref.shape)
    row = lax.broadcasted_iota(jnp.int32, (SUBLANES, tr), 0)
    d_ref[...] = jnp.where(row == 0, dests[0], jnp.where(row == 1, dests[1], 0.0)).astype(jnp.int32)


def _plan(ids8, starts_rep, tr):
    T = ids8.shape[1]
    upper = (lax.broadcasted_iota(jnp.int32, (tr, tr), 0)
             < lax.broadcasted_iota(jnp.int32, (tr, tr), 1)).astype(BF16)
    return pl.pallas_call(
        _plan_body,
        grid=(T // tr,),
        in_specs=[pl.BlockSpec((SUBLANES, tr), lambda i: (0, i)),
                  pl.BlockSpec((N_EXPERTS, LANES), lambda i: (0, 0)),
                  pl.BlockSpec((tr, tr), lambda i: (0, 0))],
        out_specs=pl.BlockSpec((SUBLANES, tr), lambda i: (0, i)),
        out_shape=jax.ShapeDtypeStruct((SUBLANES, T), jnp.int32),
        scratch_shapes=[pltpu.VMEM((N_EXPERTS, LANES), F32)],
        compiler_params=_cparams(("arbitrary",)),
        name="plan",
    )(ids8, starts_rep, upper)


def _dispatch_body(endp_ref, npad_ref, d0_ref, d1_ref, h_ref, g_ref, xs_ref, xn_ref, zb_ref, sem, zsem,
                   *, tr, tm, n_steps):
    i = pl.program_id(0)
    slot = i % 2

    def wait_rows(s):
        for _ in range(TOP_K):
            pltpu.make_async_copy(xn_ref.at[s], xs_ref.at[pl.ds(0, tr), :], sem.at[s]).wait()

    @pl.when(i == 0)
    def _():
        zb_ref[...] = jnp.zeros(zb_ref.shape, F32)

        def zero_tail(e, carry):
            @pl.when(npad_ref[e] > 0)
            def _():
                tail = pl.multiple_of(endp_ref[e] - tm, tm)
                cp = pltpu.make_async_copy(zb_ref, xs_ref.at[pl.ds(tail, tm), :], zsem)
                cp.start()
                cp.wait()
            return carry

        lax.fori_loop(0, N_EXPERTS, zero_tail, 0)

        def zero_unused(j, carry):
            cp = pltpu.make_async_copy(zb_ref, xs_ref.at[pl.ds(pl.multiple_of(j * tm, tm), tm), :], zsem)
            cp.start()
            cp.wait()
            return carry

        lax.fori_loop(endp_ref[N_EXPERTS - 1] // tm, xs_ref.shape[0] // tm, zero_unused, 0)

    @pl.when(i >= 2)
    def _():
        wait_rows(slot)

    xn_ref[slot] = _rmsnorm_rows(h_ref[...], g_ref[...])

    def scatter(t, carry):
        src = xn_ref.at[slot, pl.ds(t, 1), :]
        pltpu.make_async_copy(src, xs_ref.at[pl.ds(d0_ref[t], 1), :], sem.at[slot]).start()
        pltpu.make_async_copy(src, xs_ref.at[pl.ds(d1_ref[t], 1), :], sem.at[slot]).start()
        return carry

    lax.fori_loop(0, tr, scatter, 0, unroll=8)

    @pl.when(i == n_steps - 1)
    def _():
        wait_rows(slot)
        if n_steps > 1:
            wait_rows(1 - slot)


def _dispatch(ends_p, n_pad, dest0, dest1, h1, g, n_rows, tr, tm):
    T, D = h1.shape
    n_steps = T // tr
    gs = pltpu.PrefetchScalarGridSpec(
        num_scalar_prefetch=2,
        grid=(n_steps,),
        in_specs=[pl.BlockSpec((tr,), lambda i, ep, npd: (i,), memory_space=pltpu.SMEM),
                  pl.BlockSpec((tr,), lambda i, ep, npd: (i,), memory_space=pltpu.SMEM),
                  pl.BlockSpec((tr, D), lambda i, ep, npd: (i, 0)),
                  pl.BlockSpec((1, D), lambda i, ep, npd: (0, 0))],
        out_specs=pl.BlockSpec(memory_space=pl.ANY),
        scratch_shapes=[pltpu.VMEM((2, tr, D), F32), pltpu.VMEM((tm, D), F32),
                        pltpu.SemaphoreType.DMA((2,)), pltpu.SemaphoreType.DMA(())],
    )
    return pl.pallas_call(
        functools.partial(_dispatch_body, tr=tr, tm=tm, n_steps=n_steps),
        grid_spec=gs,
        out_shape=jax.ShapeDtypeStruct((n_rows, D), F32),
        compiler_params=_cparams(("arbitrary",)),
        name="dispatch",
    )(ends_p, n_pad, dest0, dest1, h1, g.reshape(1, D))


def _moe_body(te_ref, nu_ref, ne_ref, cb_ref, ce_ref, sl_ref,
              x_ref, wg_hbm, wu_hbm, wd_hbm, y_ref,
              wg_s, wu_s, wd_s, sg, su, sd, sem):
    i = pl.program_id(0)
    slot = sl_ref[i]
    fu = wg_s.shape[-1] // W_UNITS

    def unit_copies(e, c, par):
        col = pl.multiple_of(c * fu, fu)
        return (pltpu.make_async_copy(wg_hbm.at[e, :, pl.ds(col, fu)], sg.at[par], sem.at[par]),
                pltpu.make_async_copy(wu_hbm.at[e, :, pl.ds(col, fu)], su.at[par], sem.at[par]),
                pltpu.make_async_copy(wd_hbm.at[e, pl.ds(col, fu), :], sd.at[par], sem.at[par]))

    def start_unit(e, c, par):
        for cp in unit_copies(e, c, par):
            cp.start()

    def wait_unit(e, c, par):
        for cp in unit_copies(e, c, par):
            cp.wait()

    def convert_unit(c, par, dst_slot):
        col = pl.multiple_of(c * fu, fu)
        wg_s[dst_slot, :, pl.ds(col, fu)] = sg[par].astype(BF16)
        wu_s[dst_slot, :, pl.ds(col, fu)] = su[par].astype(BF16)
        wd_s[dst_slot, pl.ds(col, fu), :] = sd[par].astype(BF16)

    def load_units(e, c_lo, c_hi, dst_slot):
        def unit(c, carry):
            par = c % 2

            @pl.when(c == 0)
            def _():
                start_unit(e, 0, 0)

            wait_unit(e, c, par)

            @pl.when(c + 1 < W_UNITS)
            def _():
                start_unit(e, c + 1, 1 - par)

            convert_unit(c, par, dst_slot)
            return carry

        lax.fori_loop(c_lo, c_hi, unit, 0)

    @pl.when(i == 0)
    def _():
        load_units(te_ref[0], 0, W_UNITS, slot)

    @pl.when(i < nu_ref[0])
    def _():
        x = x_ref[...].astype(BF16)
        a = jnp.dot(x, wg_s[slot], preferred_element_type=F32)
        b = jnp.dot(x, wu_s[slot], preferred_element_type=F32)
        hid = (a * jax.nn.sigmoid(a) * b).astype(BF16)
        y_ref[...] = jnp.dot(hid, wd_s[slot], preferred_element_type=F32)

    @pl.when(i >= nu_ref[0])
    def _():
        y_ref[...] = jnp.zeros(y_ref.shape, y_ref.dtype)

    load_units(ne_ref[i], cb_ref[i], ce_ref[i], 1 - slot)


def _moe(plan, x_sorted, w_gate, w_up, w_down, tm):
    P, D = x_sorted.shape
    F = w_gate.shape[-1]
    fu = F // W_UNITS
    n_pref = len(plan)
    x_map = lambda i, te, nu, *_: (jnp.minimum(i, nu[0] - 1), 0)
    hbm = pl.BlockSpec(memory_space=pl.ANY)
    gs = pltpu.PrefetchScalarGridSpec(
        num_scalar_prefetch=n_pref,
        grid=(P // tm,),
        in_specs=[pl.BlockSpec((tm, D), x_map), hbm, hbm, hbm],
        out_specs=pl.BlockSpec((tm, D), lambda i, *_: (i, 0)),
        scratch_shapes=[pltpu.VMEM((2, D, F), BF16), pltpu.VMEM((2, D, F), BF16), pltpu.VMEM((2, F, D), BF16),
                        pltpu.VMEM((2, D, fu), F32), pltpu.VMEM((2, D, fu), F32), pltpu.VMEM((2, fu, D), F32),
                        pltpu.SemaphoreType.DMA((2,))],
    )
    return pl.pallas_call(
        _moe_body,
        grid_spec=gs,
        out_shape=jax.ShapeDtypeStruct((P, D), F32),
        compiler_params=_cparams(("arbitrary",)),
        name="moe",
    )(*plan, x_sorted, w_gate, w_up, w_down)


def _combine_body(d0_ref, d1_ref, h_ref, gw_ref, g_ref, ys_hbm, o_ref, yb_ref, sem, *, tr, th, final_norm):
    halves = tr // th

    def gather(hf):
        def one(t, carry):
            tok = hf * th + t
            pltpu.make_async_copy(ys_hbm.at[pl.ds(d0_ref[tok], 1), :], yb_ref.at[hf, 0, pl.ds(t, 1), :],
                                  sem.at[hf]).start()
            pltpu.make_async_copy(ys_hbm.at[pl.ds(d1_ref[tok], 1), :], yb_ref.at[hf, 1, pl.ds(t, 1), :],
                                  sem.at[hf]).start()
            return carry

        lax.fori_loop(0, th, one, 0, unroll=8)

    for hf in range(halves):
        gather(hf)
    for hf in range(halves):
        for k in range(TOP_K):
            pltpu.make_async_copy(ys_hbm.at[pl.ds(0, th), :], yb_ref.at[hf, k], sem.at[hf]).wait()
        rows = pl.ds(hf * th, th)
        gw = gw_ref[rows, :]
        h = h_ref[rows, :] + gw[:, 0:1] * yb_ref[hf, 0] + gw[:, 1:2] * yb_ref[hf, 1]
        if final_norm:
            h = _rmsnorm_rows(h, g_ref[...])
        o_ref[rows, :] = h


def _combine(dest0, dest1, h1, gwt, g, y_sorted, final_norm, tr, th):
    T, D = h1.shape
    smem = pl.BlockSpec((tr,), lambda i: (i,), memory_space=pltpu.SMEM)
    row = pl.BlockSpec((tr, D), lambda i: (i, 0))
    return pl.pallas_call(
        functools.partial(_combine_body, tr=tr, th=th, final_norm=final_norm),
        grid=(T // tr,),
        in_specs=[smem, smem, row, pl.BlockSpec((tr, LANES), lambda i: (i, 0)),
                  pl.BlockSpec((1, D), lambda i: (0, 0)), pl.BlockSpec(memory_space=pl.ANY)],
        out_specs=row,
        out_shape=jax.ShapeDtypeStruct((T, D), F32),
        scratch_shapes=[pltpu.VMEM((tr // th, TOP_K, th, D), F32), pltpu.SemaphoreType.DMA((tr // th,))],
        compiler_params=_cparams(("arbitrary",)),
        name="combine",
    )(dest0, dest1, h1, gwt, g.reshape(1, D), y_sorted)


def _router_weights(w_group, b_group, w_expert, b_expert):
    D = w_group.shape[0]
    w = jnp.zeros((D, ROUTER_COLS), F32)
    w = w.at[:, :N_GROUPS].set(w_group).at[:, EXPERT_COL0:EXPERT_COL0 + N_EXPERTS].set(w_expert)
    b = jnp.zeros((1, ROUTER_COLS), F32)
    b = b.at[0, :N_GROUPS].set(b_group).at[0, EXPERT_COL0:EXPERT_COL0 + N_EXPERTS].set(b_expert)
    w_hi = w.astype(BF16)
    w_lo = (w - w_hi.astype(F32)).astype(BF16)
    return jnp.concatenate([w_hi, w_lo], axis=1), b


def _tile_plan(counts, tm, n_tiles):
    i32 = jnp.int32
    eidx = jnp.arange(N_EXPERTS, dtype=i32)
    tiles_e = (counts + tm - 1) // tm
    n_pad = tiles_e * tm
    ends_p = jnp.cumsum(n_pad).astype(i32)
    starts_p = ends_p - n_pad
    cum_tiles = jnp.cumsum(tiles_e).astype(i32)
    n_used = cum_tiles[-1]
    used = tiles_e > 0
    last_e = jnp.max(jnp.where(used, eidx, 0))
    ti = jnp.arange(n_tiles, dtype=i32)
    te = jnp.minimum(jnp.sum((ti[:, None] >= cum_tiles[None, :]).astype(i32), axis=1), last_e)
    t_in = ti - (cum_tiles - tiles_e)[te]
    t_n = jnp.maximum(tiles_e[te], 1)
    nxt = jnp.min(jnp.where(used[None, :] & (eidx[None, :] > eidx[:, None]), eidx[None, :], N_EXPERTS), axis=1)
    has_next = (nxt < N_EXPERTS)[te] & (ti < n_used)
    ne = jnp.where(has_next, nxt[te], 0).astype(i32)
    cb = jnp.where(has_next, (W_UNITS * t_in) // t_n, 0).astype(i32)
    ce = jnp.where(has_next, (W_UNITS * (t_in + 1)) // t_n, 0).astype(i32)
    slot = ((jnp.cumsum(used.astype(i32)) - 1)[te] % 2).astype(i32)
    moe_plan = (te.astype(i32), n_used.reshape(1).astype(i32), ne, cb, ce, slot)
    return starts_p, ends_p, n_pad.astype(i32), moe_plan


def kernel(x, norm_mix_g, w_in, lambda_q1, lambda_k1, lambda_q2, lambda_k2, subln_g, conv_w, conv_b,
           conv_ln_g, conv_ln_b, w_out, norm_ffn_g, w_group_router, b_group_router, w_expert_router,
           b_expert_router, w_gate, w_up, w_down, norm_final_g):
    B, S, D = x.shape
    T = B * S
    depth = w_in.shape[0]
    tm_proj = min(1024, T)
    tm_out = min(512, T)
    tq = min(512, S)
    ts = min(512, S)
    tm_moe = 256
    tr = min(1024, T)
    slopes = 2.0 ** (-8.0 * (jnp.arange(N_HEADS, dtype=F32) + 1.0) / N_HEADS)

    h = x.reshape(T, D)
    for l in range(depth):
        lam0 = 0.8 - 0.6 * math.exp(-0.3 * l)
        lamv = jnp.stack([lambda_q1[l], lambda_k1[l], lambda_q2[l], lambda_k2[l]]).astype(F32)
        proj = _inproj(h, norm_mix_g[l], w_in[l].astype(BF16), tm_proj, 1024)
        proj3 = proj.reshape(B, S, proj.shape[1])
        attn = _attention(proj3, slopes, lamv, subln_g[l], lam0, tq)
        conv = _conv(proj3, conv_w[l], conv_b[l], conv_ln_g[l], conv_ln_b[l], ts, 64)
        w_o = w_out[l].astype(BF16)
        wr, br = _router_weights(w_group_router[l], b_group_router[l], w_expert_router[l], b_expert_router[l])
        h1, logits = _outproj(h, attn.reshape(T, ATTN_WIDTH), conv.reshape(T, -1),
                              w_o[:ATTN_WIDTH], w_o[ATTN_WIDTH:], norm_ffn_g[l], wr, br, tm_out)
        ids8, gwt, cnt = _route(logits, tr)
        n_rows = T * TOP_K + N_EXPERTS * tm_moe
        starts_p, ends_p, n_pad, moe_plan = _tile_plan(cnt[:, 0].astype(jnp.int32), tm_moe, n_rows // tm_moe)
        starts_rep = jnp.broadcast_to(starts_p.astype(F32)[:, None], (N_EXPERTS, LANES))
        dest8 = _plan(ids8, starts_rep, tr)
        dest0, dest1 = dest8[0], dest8[1]
        x_sorted = _dispatch(ends_p, n_pad, dest0, dest1, h1, norm_ffn_g[l], n_rows, tr, tm_moe)
        y_sorted = _moe(moe_plan, x_sorted, w_gate[l], w_up[l], w_down[l], tm_moe)
        h = _combine(dest0, dest1, h1, gwt, norm_final_g, y_sorted, l == depth - 1, tr, tr // 2)
    return h.reshape(B, S, D)
```

```python
import functools
import math

import jax
import jax.numpy as jnp
from jax import lax
from jax.experimental import pallas as pl
from jax.experimental.pallas import tpu as pltpu

N_HEADS = 8
QK_DIM = 64
HEAD_DIM = 2 * QK_DIM
ATTN_WIDTH = N_HEADS * HEAD_DIM
CONV_TAPS = 31
CONV_HALO = 32
N_GROUPS = 4
EXPERTS_PER_GROUP = 8
N_EXPERTS = N_GROUPS * EXPERTS_PER_GROUP
TOP_K = 2
EPS = 1e-6
NEG_INF = -1e30
LOG2E = 1.4426950408889634
LANES = 128
SUBLANES = 8
ROUTER_COLS = LANES
EXPERT_COL0 = SUBLANES
W_UNITS = 8
V7X_VMEM_LIMIT = 56 * 1024 * 1024

F32 = jnp.float32
BF16 = jnp.bfloat16


def _cparams(sem):
    return pltpu.CompilerParams(dimension_semantics=sem, vmem_limit_bytes=V7X_VMEM_LIMIT)


def _inproj_body(x_ref, g_ref, w_ref, o_ref, xn_ref):
    @pl.when(pl.program_id(1) == 0)
    def _():
        x = x_ref[...]
        ms = jnp.mean(x * x, axis=-1, keepdims=True)
        xn_ref[...] = (x * lax.rsqrt(ms + EPS) * g_ref[...]).astype(BF16)

    o_ref[...] = jnp.dot(xn_ref[...], w_ref[...], preferred_element_type=F32).astype(o_ref.dtype)


def _inproj(x2, g, w, tm, tn):
    T, D = x2.shape
    N = w.shape[1]
    return pl.pallas_call(
        _inproj_body,
        grid=(T // tm, N // tn),
        in_specs=[pl.BlockSpec((tm, D), lambda i, j: (i, 0)),
                  pl.BlockSpec((1, D), lambda i, j: (0, 0)),
                  pl.BlockSpec((D, tn), lambda i, j: (0, j))],
        out_specs=pl.BlockSpec((tm, tn), lambda i, j: (i, j)),
        out_shape=jax.ShapeDtypeStruct((T, N), BF16),
        scratch_shapes=[pltpu.VMEM((tm, D), BF16)],
        compiler_params=_cparams(("parallel", "arbitrary")),
        name="inproj",
    )(x2, g.reshape(1, D), w)


def _attn_body(qi_ref, j_ref, slopes_ref, lamv_ref, q_ref, k_ref, v_ref, g_ref, o_ref,
               qc_ref, kb_ref, m_ref, acc_ref, sa_ref, sb_ref, *, tq, nq, lam0):
    h = pl.program_id(1)
    tk = tq
    nstep = nq * (nq + 1) // 2
    slope2 = slopes_ref[h] * LOG2E
    lv = lamv_ref[...]
    lam = (jnp.exp(jnp.sum(lv[0:1] * lv[1:2], axis=1, keepdims=True))
           - jnp.exp(jnp.sum(lv[2:3] * lv[3:4], axis=1, keepdims=True)) + lam0)

    lane = lax.broadcasted_iota(jnp.int32, (tq, HEAD_DIM), 1)
    zero = jnp.zeros((tq, HEAD_DIM), BF16)
    one_cols = jnp.where(lane < 3, 1.0, 0.0).astype(BF16)

    def prepare(i, carry):
        rows = pl.ds(pl.multiple_of(i * tq, tq), tq)
        q = (q_ref[rows, :].astype(F32) * (QK_DIM ** -0.5 * LOG2E)).astype(BF16)
        qc_ref[i, :tq, :HEAD_DIM] = jnp.where(lane < QK_DIM, q, zero)
        qc_ref[i, tq:, :HEAD_DIM] = jnp.where(lane >= QK_DIM, q, zero)
        qc_ref[i, :tq, HEAD_DIM:] = one_cols
        qc_ref[i, tq:, HEAD_DIM:] = one_cols
        bias = slope2 * (i * tq + lax.broadcasted_iota(jnp.int32, (tq, HEAD_DIM), 0)).astype(F32)
        hi = bias.astype(BF16).astype(F32)
        mid = (bias - hi).astype(BF16).astype(F32)
        lo = bias - hi - mid
        cols = jnp.where(lane == 0, hi, jnp.where(lane == 1, mid, jnp.where(lane == 2, lo, 0.0)))
        kb_ref[rows, :] = cols.astype(BF16)
        return carry

    lax.fori_loop(0, nq, prepare, 0)

    m_ref[...] = jnp.full(m_ref.shape, NEG_INF, F32)
    acc_ref[...] = jnp.zeros(acc_ref.shape, F32)
    ones = jnp.ones((tk, HEAD_DIM), BF16)

    def scores(s_ref, t):
        rows = pl.ds(pl.multiple_of(j_ref[t] * tk, tk), tk)
        kext = jnp.concatenate([k_ref[rows, :], kb_ref[rows, :]], axis=1)
        s_ref[...] = lax.dot_general(qc_ref[qi_ref[t]], kext, (((1,), (1,)), ((), ())),
                                     preferred_element_type=F32)

    def accumulate(s_ref, t, diagonal):
        rows = pl.ds(pl.multiple_of(j_ref[t] * tk, tk), tk)
        vext = jnp.concatenate([v_ref[rows, :], ones], axis=1)
        s = s_ref[...]
        if diagonal:
            kpos = lax.broadcasted_iota(jnp.int32, (1, tk), 1)
            qpos = lax.broadcasted_iota(jnp.int32, (tq, 1), 0)
            s = jnp.where(jnp.concatenate([qpos, qpos], axis=0) >= kpos, s, NEG_INF)
        m_old = m_ref[...]
        m_new = jnp.maximum(m_old, jnp.max(s, axis=1, keepdims=True))
        alpha = jnp.tile(jnp.exp2(m_old - m_new), (1, 2))
        p = jnp.exp2(s - jnp.tile(m_new, (1, tk // LANES))).astype(BF16)
        a = alpha * acc_ref[...] + jnp.dot(p, vext, preferred_element_type=F32)
        if diagonal:
            o = (a[:tq, :HEAD_DIM] / a[:tq, HEAD_DIM:] - lam * (a[tq:, :HEAD_DIM] / a[tq:, HEAD_DIM:]))
            ms = jnp.mean(o * o, axis=1, keepdims=True)
            o_ref[pl.ds(pl.multiple_of(qi_ref[t] * tq, tq), tq), :] = (
                o * lax.rsqrt(ms + EPS) * g_ref[...] * (1.0 - lam0)).astype(o_ref.dtype)
            m_ref[...] = jnp.full(m_ref.shape, NEG_INF, F32)
            acc_ref[...] = jnp.zeros(acc_ref.shape, F32)
        else:
            acc_ref[...] = a
            m_ref[...] = m_new

    def two_steps(t, diag0, diag1, trailing_scores=True):
        scores(sb_ref, t + 1)
        accumulate(sa_ref, t, diag0)
        if trailing_scores:
            scores(sa_ref, t + 2)
        accumulate(sb_ref, t + 1, diag1)

    def pair(i, carry):
        t = 2 * i
        d0 = j_ref[t] == qi_ref[t]
        d1 = j_ref[t + 1] == qi_ref[t + 1]

        @pl.when(d0)
        def _():
            two_steps(t, True, False)

        @pl.when(jnp.logical_and(jnp.logical_not(d0), d1))
        def _():
            two_steps(t, False, True)

        @pl.when(jnp.logical_not(jnp.logical_or(d0, d1)))
        def _():
            two_steps(t, False, False)

        return carry

    scores(sa_ref, 0)
    if nstep % 2 == 0:
        lax.fori_loop(0, nstep // 2 - 1, pair, 0)
        two_steps(nstep - 2, False, True, trailing_scores=False)
    else:
        lax.fori_loop(0, nstep // 2, pair, 0)
        accumulate(sa_ref, nstep - 1, True)


def _attention(proj3, slopes, lamv, subln_g, lam0, tq):
    B, S, _ = proj3.shape
    nq = S // tq
    steps = [(qi, j) for qi in range(nq) for j in range(qi + 1)]
    qi_tab = jnp.asarray([s[0] for s in steps], jnp.int32)
    j_tab = jnp.asarray([s[1] for s in steps], jnp.int32)
    body = functools.partial(_attn_body, tq=tq, nq=nq, lam0=lam0)
    seq = lambda col0: pl.BlockSpec((None, S, HEAD_DIM), lambda b, h, *_: (b, 0, col0 + h))
    gs = pltpu.PrefetchScalarGridSpec(
        num_scalar_prefetch=2,
        grid=(B, N_HEADS),
        in_specs=[pl.BlockSpec(memory_space=pltpu.SMEM),
                  pl.BlockSpec((4, QK_DIM), lambda b, h, *_: (0, 0)),
                  seq(0), seq(N_HEADS), seq(2 * N_HEADS),
                  pl.BlockSpec((1, HEAD_DIM), lambda b, h, *_: (0, 0))],
        out_specs=seq(0),
        scratch_shapes=[pltpu.VMEM((nq, 2 * tq, 2 * HEAD_DIM), BF16), pltpu.VMEM((S, HEAD_DIM), BF16),
                        pltpu.VMEM((2 * tq, LANES), F32), pltpu.VMEM((2 * tq, 2 * HEAD_DIM), F32),
                        pltpu.VMEM((2 * tq, tq), F32), pltpu.VMEM((2 * tq, tq), F32)],
    )
    return pl.pallas_call(
        body,
        grid_spec=gs,
        out_shape=jax.ShapeDtypeStruct((B, S, ATTN_WIDTH), BF16),
        compiler_params=_cparams(("parallel", "arbitrary")),
        name="diffattn",
    )(qi_tab, j_tab, slopes, lamv, proj3, proj3, proj3, subln_g.reshape(1, HEAD_DIM))


def _conv_body(a_ref, g_ref, ah_ref, gh_ref, w_ref, cb_ref, lng_ref, lnb_ref, o_ref, u_ref, c_ref,
               *, ts, rb):
    i = pl.program_id(1)
    nchunk = u_ref.shape[0]
    u = a_ref[...].astype(F32) * jax.nn.sigmoid(g_ref[...].astype(F32))
    uh = ah_ref[...].astype(F32) * jax.nn.sigmoid(gh_ref[...].astype(F32))
    uh = jnp.where(i > 0, uh, 0.0)
    for c in range(nchunk):
        u_ref[c, pl.ds(0, CONV_HALO), :] = uh[:, c * LANES:(c + 1) * LANES]
        u_ref[c, pl.ds(CONV_HALO, ts), :] = u[:, c * LANES:(c + 1) * LANES]

    first = CONV_HALO - (CONV_TAPS - 1)

    def lane_chunk(c, carry):
        for r0 in range(0, ts, rb):
            acc = jnp.zeros((rb, LANES), F32)
            for k in range(CONV_TAPS):
                acc = acc + jnp.tile(w_ref[c, k], (rb // SUBLANES, 1)) * u_ref[c, pl.ds(r0 + first + k, rb), :]
            c_ref[c, pl.ds(r0, rb), :] = acc
        return carry

    lax.fori_loop(0, nchunk, lane_chunk, 0)

    C = nchunk * LANES
    cs = [c_ref[c] + cb_ref[:, c * LANES:(c + 1) * LANES] for c in range(nchunk)]
    mean = jnp.sum(functools.reduce(jnp.add, cs), axis=1, keepdims=True) * (1.0 / C)
    xcs = [x - mean for x in cs]
    var = jnp.sum(functools.reduce(jnp.add, [x * x for x in xcs]), axis=1, keepdims=True) * (1.0 / C)
    rstd = lax.rsqrt(var + EPS)
    for c in range(nchunk):
        sl = slice(c * LANES, (c + 1) * LANES)
        y = xcs[c] * rstd * lng_ref[:, sl] + lnb_ref[:, sl]
        o_ref[:, sl] = (y * jax.nn.sigmoid(y)).astype(o_ref.dtype)


def _conv(proj3, conv_w, conv_b, ln_g, ln_b, ts, rb):
    B, S, _ = proj3.shape
    C = conv_w.shape[-1]
    nchunk = C // LANES
    a_blk = (3 * ATTN_WIDTH) // C
    hpt = ts // CONV_HALO
    body = functools.partial(_conv_body, ts=ts, rb=rb)
    w_b = jnp.broadcast_to(conv_w.reshape(CONV_TAPS, nchunk, 1, LANES).transpose(1, 0, 2, 3),
                           (nchunk, CONV_TAPS, SUBLANES, LANES))
    halo = lambda blk: (lambda b, i: (b, jnp.maximum(i * hpt - 1, 0), blk))
    vec = pl.BlockSpec((1, C), lambda b, i: (0, 0))
    return pl.pallas_call(
        body,
        grid=(B, S // ts),
        in_specs=[pl.BlockSpec((None, ts, C), lambda b, i: (b, i, a_blk)),
                  pl.BlockSpec((None, ts, C), lambda b, i: (b, i, a_blk + 1)),
                  pl.BlockSpec((None, CONV_HALO, C), halo(a_blk)),
                  pl.BlockSpec((None, CONV_HALO, C), halo(a_blk + 1)),
                  pl.BlockSpec((nchunk, CONV_TAPS, SUBLANES, LANES), lambda b, i: (0, 0, 0, 0)),
                  vec, vec, vec],
        out_specs=pl.BlockSpec((None, ts, C), lambda b, i: (b, i, 0)),
        out_shape=jax.ShapeDtypeStruct((B, S, C), BF16),
        scratch_shapes=[pltpu.VMEM((nchunk, ts + CONV_HALO, LANES), F32), pltpu.VMEM((nchunk, ts, LANES), F32)],
        compiler_params=_cparams(("parallel", "arbitrary")),
        name="convmod",
    )(proj3, proj3, proj3, proj3, w_b, conv_b.reshape(1, C), ln_g.reshape(1, C), ln_b.reshape(1, C))


def _rmsnorm_rows(h, g):
    ms = jnp.mean(h * h, axis=-1, keepdims=True)
    return h * lax.rsqrt(ms + EPS) * g


def _outproj_body(x_ref, a_ref, c_ref, wa_ref, wc_ref, g_ref, wr_ref, br_ref, h_ref, lg_ref):
    acc = (jnp.dot(a_ref[...], wa_ref[...], preferred_element_type=F32)
           + jnp.dot(c_ref[...], wc_ref[...], preferred_element_type=F32))
    h = x_ref[...] + acc
    h_ref[...] = h
    xn = _rmsnorm_rows(h, g_ref[...])
    hi = xn.astype(BF16)
    lo = (xn - hi.astype(F32)).astype(BF16)
    r = (jnp.dot(hi, wr_ref[...], preferred_element_type=F32)
         + jnp.dot(lo, wr_ref[...], preferred_element_type=F32))
    lg_ref[...] = r[:, :ROUTER_COLS] + r[:, ROUTER_COLS:] + br_ref[...]


def _outproj(x2, attn2, conv2, w_attn, w_conv, g, wr, br, tm):
    T, D = x2.shape
    Ka, Kc = attn2.shape[1], conv2.shape[1]
    row = lambda n: pl.BlockSpec((tm, n), lambda i: (i, 0))
    full = lambda a, b: pl.BlockSpec((a, b), lambda i: (0, 0))
    return pl.pallas_call(
        _outproj_body,
        grid=(T // tm,),
        in_specs=[row(D), row(Ka), row(Kc), full(Ka, D), full(Kc, D), full(1, D),
                  full(D, 2 * ROUTER_COLS), full(1, ROUTER_COLS)],
        out_specs=[row(D), row(ROUTER_COLS)],
        out_shape=[jax.ShapeDtypeStruct((T, D), F32), jax.ShapeDtypeStruct((T, ROUTER_COLS), F32)],
        compiler_params=_cparams(("parallel",)),
        name="outproj",
    )(x2, attn2, conv2, w_attn, w_conv, g.reshape(1, D), wr, br)


def _route_body(lg_ref, id_ref, gw_ref, cnt_ref):
    lt = lg_ref[...].T
    tr = lt.shape[1]
    row = lax.broadcasted_iota(jnp.int32, (SUBLANES, tr), 0)
    gl = jnp.where(row < N_GROUPS, lt[0:SUBLANES], -jnp.inf)
    gmax = jnp.max(gl, axis=0, keepdims=True)
    g_sel = jnp.min(jnp.where(gl == gmax, row, SUBLANES), axis=0, keepdims=True)
    g_w = 1.0 / jnp.sum(jnp.exp(gl - gmax), axis=0, keepdims=True)
    e_in = lt[EXPERT_COL0:EXPERT_COL0 + EXPERTS_PER_GROUP]
    for g in range(1, N_GROUPS):
        lo = EXPERT_COL0 + g * EXPERTS_PER_GROUP
        e_in = jnp.where(g_sel == g, lt[lo:lo + EXPERTS_PER_GROUP], e_in)
    v1 = jnp.max(e_in, axis=0, keepdims=True)
    i1 = jnp.min(jnp.where(e_in == v1, row, SUBLANES), axis=0, keepdims=True)
    e_rest = jnp.where(row == i1, -jnp.inf, e_in)
    v2 = jnp.max(e_rest, axis=0, keepdims=True)
    i2 = jnp.min(jnp.where(e_rest == v2, row, SUBLANES), axis=0, keepdims=True)
    d = jnp.exp(v2 - v1)
    w1 = g_w / (1.0 + d)
    w2 = w1 * d
    e1 = g_sel * EXPERTS_PER_GROUP + i1
    e2 = g_sel * EXPERTS_PER_GROUP + i2
    id_ref[...] = jnp.where(row == 0, e1, jnp.where(row == 1, e2, 0))
    rowl = lax.broadcasted_iota(jnp.int32, (LANES, tr), 0)
    gw_ref[...] = jnp.where(rowl == 0, w1, jnp.where(rowl == 1, w2, 0.0)).T
    rowe = lax.broadcasted_iota(jnp.int32, (N_EXPERTS, tr), 0)
    hits = jnp.where(rowe == e1, 1.0, 0.0) + jnp.where(rowe == e2, 1.0, 0.0)

    @pl.when(pl.program_id(0) == 0)
    def _():
        cnt_ref[...] = jnp.zeros(cnt_ref.shape, F32)

    cnt_ref[...] += jnp.broadcast_to(jnp.sum(hits, axis=1, keepdims=True), cnt_ref.shape)


def _route(logits, tr):
    T = logits.shape[0]
    return pl.pallas_call(
        _route_body,
        grid=(T // tr,),
        in_specs=[pl.BlockSpec((tr, ROUTER_COLS), lambda i: (i, 0))],
        out_specs=[pl.BlockSpec((SUBLANES, tr), lambda i: (0, i)),
                   pl.BlockSpec((tr, LANES), lambda i: (i, 0)),
                   pl.BlockSpec((N_EXPERTS, LANES), lambda i: (0, 0))],
        out_shape=[jax.ShapeDtypeStruct((SUBLANES, T), jnp.int32),
                   jax.ShapeDtypeStruct((T, LANES), F32),
                   jax.ShapeDtypeStruct((N_EXPERTS, LANES), F32)],
        compiler_params=_cparams(("arbitrary",)),
        name="route",
    )(logits)


def _plan_body(id_ref, st_ref, u_ref, d_ref, base_ref):
    @pl.when(pl.program_id(0) == 0)
    def _():
        base_ref[...] = st_ref[...]

    ids = id_ref[...]
    tr = ids.shape[1]
    rowe = lax.broadcasted_iota(jnp.int32, (N_EXPERTS, tr), 0)
    base = base_ref[:, 0:1]
    dests = []
    for k in range(TOP_K):
        hit = rowe == ids[k:k + 1]
        before = jnp.dot(jnp.where(hit, 1.0, 0.0).astype(BF16), u_ref[...], preferred_element_type=F32)
        dests.append(jnp.sum(jnp.where(hit, base + before, 0.0), axis=0, keepdims=True))
        base = base + jnp.sum(jnp.where(hit, 1.0, 0.0), axis=1, keepdims=True)
    base_ref[...] = jnp.broadcast_to(base, base_ref.shape)
    row = lax.broadcasted_iota(jnp.int32, (SUBLANES, tr), 0)
    d_ref[...] = jnp.where(row == 0, dests[0], jnp.where(row == 1, dests[1], 0.0)).astype(jnp.int32)


def _plan(ids8, starts_rep, tr):
    T = ids8.shape[1]
    upper = (lax.broadcasted_iota(jnp.int32, (tr, tr), 0)
             < lax.broadcasted_iota(jnp.int32, (tr, tr), 1)).astype(BF16)
    return pl.pallas_call(
        _plan_body,
        grid=(T // tr,),
        in_specs=[pl.BlockSpec((SUBLANES, tr), lambda i: (0, i)),
                  pl.BlockSpec((N_EXPERTS, LANES), lambda i: (0, 0)),
                  pl.BlockSpec((tr, tr), lambda i: (0, 0))],
        out_specs=pl.BlockSpec((SUBLANES, tr), lambda i: (0, i)),
        out_shape=jax.ShapeDtypeStruct((SUBLANES, T), jnp.int32),
        scratch_shapes=[pltpu.VMEM((N_EXPERTS, LANES), F32)],
        compiler_params=_cparams(("arbitrary",)),
        name="plan",
    )(ids8, starts_rep, upper)


def _dispatch_body(endp_ref, npad_ref, d0_ref, d1_ref, h_ref, g_ref, xs_ref, xn_ref, zb_ref, sem, zsem,
                   *, tr, tm, n_steps):
    i = pl.program_id(0)
    slot = i % 2

    def wait_rows(s):
        for _ in range(TOP_K):
            pltpu.make_async_copy(xn_ref.at[s], xs_ref.at[pl.ds(0, tr), :], sem.at[s]).wait()

    @pl.when(i == 0)
    def _():
        zb_ref[...] = jnp.zeros(zb_ref.shape, F32)

        def zero_tail(e, carry):
            @pl.when(npad_ref[e] > 0)
            def _():
                tail = pl.multiple_of(endp_ref[e] - tm, tm)
                cp = pltpu.make_async_copy(zb_ref, xs_ref.at[pl.ds(tail, tm), :], zsem)
                cp.start()
                cp.wait()
            return carry

        lax.fori_loop(0, N_EXPERTS, zero_tail, 0)

        def zero_unused(j, carry):
            cp = pltpu.make_async_copy(zb_ref, xs_ref.at[pl.ds(pl.multiple_of(j * tm, tm), tm), :], zsem)
            cp.start()
            cp.wait()
            return carry

        lax.fori_loop(endp_ref[N_EXPERTS - 1] // tm, xs_ref.shape[0] // tm, zero_unused, 0)

    @pl.when(i >= 2)
    def _():
        wait_rows(slot)

    xn_ref[slot] = _rmsnorm_rows(h_ref[...], g_ref[...])

    def scatter(t, carry):
        src = xn_ref.at[slot, pl.ds(t, 1), :]
        pltpu.make_async_copy(src, xs_ref.at[pl.ds(d0_ref[t], 1), :], sem.at[slot]).start()
        pltpu.make_async_copy(src, xs_ref.at[pl.ds(d1_ref[t], 1), :], sem.at[slot]).start()
        return carry

    lax.fori_loop(0, tr, scatter, 0, unroll=8)

    @pl.when(i == n_steps - 1)
    def _():
        wait_rows(slot)
        if n_steps > 1:
            wait_rows(1 - slot)


def _dispatch(ends_p, n_pad, dest0, dest1, h1, g, n_rows, tr, tm):
    T, D = h1.shape
    n_steps = T // tr
    gs = pltpu.PrefetchScalarGridSpec(
        num_scalar_prefetch=2,
        grid=(n_steps,),
        in_specs=[pl.BlockSpec((tr,), lambda i, ep, npd: (i,), memory_space=pltpu.SMEM),
                  pl.BlockSpec((tr,), lambda i, ep, npd: (i,), memory_space=pltpu.SMEM),
                  pl.BlockSpec((tr, D), lambda i, ep, npd: (i, 0)),
                  pl.BlockSpec((1, D), lambda i, ep, npd: (0, 0))],
        out_specs=pl.BlockSpec(memory_space=pl.ANY),
        scratch_shapes=[pltpu.VMEM((2, tr, D), F32), pltpu.VMEM((tm, D), F32),
                        pltpu.SemaphoreType.DMA((2,)), pltpu.SemaphoreType.DMA(())],
    )
    return pl.pallas_call(
        functools.partial(_dispatch_body, tr=tr, tm=tm, n_steps=n_steps),
        grid_spec=gs,
        out_shape=jax.ShapeDtypeStruct((n_rows, D), F32),
        compiler_params=_cparams(("arbitrary",)),
        name="dispatch",
    )(ends_p, n_pad, dest0, dest1, h1, g.reshape(1, D))


def _moe_body(te_ref, nu_ref, ne_ref, cb_ref, ce_ref, sl_ref,
              x_ref, wg_hbm, wu_hbm, wd_hbm, y_ref,
              wg_s, wu_s, wd_s, sg, su, sd, sem):
    i = pl.program_id(0)
    slot = sl_ref[i]
    fu = wg_s.shape[-1] // W_UNITS

    def unit_copies(e, c, par):
        col = pl.multiple_of(c * fu, fu)
        return (pltpu.make_async_copy(wg_hbm.at[e, :, pl.ds(col, fu)], sg.at[par], sem.at[par]),
                pltpu.make_async_copy(wu_hbm.at[e, :, pl.ds(col, fu)], su.at[par], sem.at[par]),
                pltpu.make_async_copy(wd_hbm.at[e, pl.ds(col, fu), :], sd.at[par], sem.at[par]))

    def start_unit(e, c, par):
        for cp in unit_copies(e, c, par):
            cp.start()

    def wait_unit(e, c, par):
        for cp in unit_copies(e, c, par):
            cp.wait()

    def convert_unit(c, par, dst_slot):
        col = pl.multiple_of(c * fu, fu)
        wg_s[dst_slot, :, pl.ds(col, fu)] = sg[par].astype(BF16)
        wu_s[dst_slot, :, pl.ds(col, fu)] = su[par].astype(BF16)
        wd_s[dst_slot, pl.ds(col, fu), :] = sd[par].astype(BF16)

    def load_units(e, c_lo, c_hi, dst_slot):
        def unit(c, carry):
            par = c % 2
            wait_unit(e, c, par)

            @pl.when(c + 1 < W_UNITS)
            def _():
                start_unit(e, c + 1, 1 - par)

            convert_unit(c, par, dst_slot)
            return carry

        lax.fori_loop(c_lo, c_hi, unit, 0)

    @pl.when(i == 0)
    def _():
        start_unit(te_ref[0], 0, 0)
        load_units(te_ref[0], 0, W_UNITS, slot)

    @pl.when(jnp.logical_and(cb_ref[i] == 0, ce_ref[i] > 0))
    def _():
        start_unit(ne_ref[i], 0, 0)

    @pl.when(i < nu_ref[0])
    def _():
        x = x_ref[...].astype(BF16)
        a = jnp.dot(x, wg_s[slot], preferred_element_type=F32)
        b = jnp.dot(x, wu_s[slot], preferred_element_type=F32)
        hid = (a * jax.nn.sigmoid(a) * b).astype(BF16)
        y_ref[...] = jnp.dot(hid, wd_s[slot], preferred_element_type=F32)

    @pl.when(i >= nu_ref[0])
    def _():
        y_ref[...] = jnp.zeros(y_ref.shape, y_ref.dtype)

    load_units(ne_ref[i], cb_ref[i], ce_ref[i], 1 - slot)


def _moe(plan, x_sorted, w_gate, w_up, w_down, tm):
    P, D = x_sorted.shape
    F = w_gate.shape[-1]
    fu = F // W_UNITS
    n_pref = len(plan)
    x_map = lambda i, te, nu, *_: (jnp.minimum(i, nu[0] - 1), 0)
    hbm = pl.BlockSpec(memory_space=pl.ANY)
    gs = pltpu.PrefetchScalarGridSpec(
        num_scalar_prefetch=n_pref,
        grid=(P // tm,),
        in_specs=[pl.BlockSpec((tm, D), x_map), hbm, hbm, hbm],
        out_specs=pl.BlockSpec((tm, D), lambda i, *_: (i, 0)),
        scratch_shapes=[pltpu.VMEM((2, D, F), BF16), pltpu.VMEM((2, D, F), BF16), pltpu.VMEM((2, F, D), BF16),
                        pltpu.VMEM((2, D, fu), F32), pltpu.VMEM((2, D, fu), F32), pltpu.VMEM((2, fu, D), F32),
                        pltpu.SemaphoreType.DMA((2,))],
    )
    return pl.pallas_call(
        _moe_body,
        grid_spec=gs,
        out_shape=jax.ShapeDtypeStruct((P, D), F32),
        compiler_params=_cparams(("arbitrary",)),
        name="moe",
    )(*plan, x_sorted, w_gate, w_up, w_down)


def _combine_body(d0_ref, d1_ref, h_ref, gw_ref, g_ref, ys_hbm, o_ref, yb_ref, sem, *, tr, th, final_norm):
    halves = tr // th

    def gather(hf):
        def one(t, carry):
            tok = hf * th + t
            pltpu.make_async_copy(ys_hbm.at[pl.ds(d0_ref[tok], 1), :], yb_ref.at[hf, 0, pl.ds(t, 1), :],
                                  sem.at[hf]).start()
            pltpu.make_async_copy(ys_hbm.at[pl.ds(d1_ref[tok], 1), :], yb_ref.at[hf, 1, pl.ds(t, 1), :],
                                  sem.at[hf]).start()
            return carry

        lax.fori_loop(0, th, one, 0, unroll=8)

    for hf in range(halves):
        gather(hf)
    for hf in range(halves):
        for k in range(TOP_K):
            pltpu.make_async_copy(ys_hbm.at[pl.ds(0, th), :], yb_ref.at[hf, k], sem.at[hf]).wait()
        rows = pl.ds(hf * th, th)
        gw = gw_ref[rows, :]
        h = h_ref[rows, :] + gw[:, 0:1] * yb_ref[hf, 0] + gw[:, 1:2] * yb_ref[hf, 1]
        if final_norm:
            h = _rmsnorm_rows(h, g_ref[...])
        o_ref[rows, :] = h


def _combine(dest0, dest1, h1, gwt, g, y_sorted, final_norm, tr, th):
    T, D = h1.shape
    smem = pl.BlockSpec((tr,), lambda i: (i,), memory_space=pltpu.SMEM)
    row = pl.BlockSpec((tr, D), lambda i: (i, 0))
    return pl.pallas_call(
        functools.partial(_combine_body, tr=tr, th=th, final_norm=final_norm),
        grid=(T // tr,),
        in_specs=[smem, smem, row, pl.BlockSpec((tr, LANES), lambda i: (i, 0)),
                  pl.BlockSpec((1, D), lambda i: (0, 0)), pl.BlockSpec(memory_space=pl.ANY)],
        out_specs=row,
        out_shape=jax.ShapeDtypeStruct((T, D), F32),
        scratch_shapes=[pltpu.VMEM((tr // th, TOP_K, th, D), F32), pltpu.SemaphoreType.DMA((tr // th,))],
        compiler_params=_cparams(("arbitrary",)),
        name="combine",
    )(dest0, dest1, h1, gwt, g.reshape(1, D), y_sorted)


def _router_weights(w_group, b_group, w_expert, b_expert):
    D = w_group.shape[0]
    w = jnp.zeros((D, ROUTER_COLS), F32)
    w = w.at[:, :N_GROUPS].set(w_group).at[:, EXPERT_COL0:EXPERT_COL0 + N_EXPERTS].set(w_expert)
    b = jnp.zeros((1, ROUTER_COLS), F32)
    b = b.at[0, :N_GROUPS].set(b_group).at[0, EXPERT_COL0:EXPERT_COL0 + N_EXPERTS].set(b_expert)
    w_hi = w.astype(BF16)
    w_lo = (w - w_hi.astype(F32)).astype(BF16)
    return jnp.concatenate([w_hi, w_lo], axis=1), b


def _tile_plan(counts, tm, n_tiles):
    i32 = jnp.int32
    eidx = jnp.arange(N_EXPERTS, dtype=i32)
    tiles_e = (counts + tm - 1) // tm
    n_pad = tiles_e * tm
    ends_p = jnp.cumsum(n_pad).astype(i32)
    starts_p = ends_p - n_pad
    cum_tiles = jnp.cumsum(tiles_e).astype(i32)
    n_used = cum_tiles[-1]
    used = tiles_e > 0
    last_e = jnp.max(jnp.where(used, eidx, 0))
    ti = jnp.arange(n_tiles, dtype=i32)
    te = jnp.minimum(jnp.sum((ti[:, None] >= cum_tiles[None, :]).astype(i32), axis=1), last_e)
    t_in = ti - (cum_tiles - tiles_e)[te]
    t_n = jnp.maximum(tiles_e[te], 1)
    nxt = jnp.min(jnp.where(used[None, :] & (eidx[None, :] > eidx[:, None]), eidx[None, :], N_EXPERTS), axis=1)
    has_next = (nxt < N_EXPERTS)[te] & (ti < n_used)
    ne = jnp.where(has_next, nxt[te], 0).astype(i32)
    cb = jnp.where(has_next, (W_UNITS * t_in) // t_n, 0).astype(i32)
    ce = jnp.where(has_next, (W_UNITS * (t_in + 1)) // t_n, 0).astype(i32)
    slot = ((jnp.cumsum(used.astype(i32)) - 1)[te] % 2).astype(i32)
    moe_plan = (te.astype(i32), n_used.reshape(1).astype(i32), ne, cb, ce, slot)
    return starts_p, ends_p, n_pad.astype(i32), moe_plan


def kernel(x, norm_mix_g, w_in, lambda_q1, lambda_k1, lambda_q2, lambda_k2, subln_g, conv_w, conv_b,
           conv_ln_g, conv_ln_b, w_out, norm_ffn_g, w_group_router, b_group_router, w_expert_router,
           b_expert_router, w_gate, w_up, w_down, norm_final_g):
    B, S, D = x.shape
    T = B * S
    depth = w_in.shape[0]
    tm_proj = min(1024, T)
    tm_out = min(512, T)
    tq = min(512, S)
    ts = min(512, S)
    tm_moe = 256
    tr = min(1024, T)
    slopes = 2.0 ** (-8.0 * (jnp.arange(N_HEADS, dtype=F32) + 1.0) / N_HEADS)

    h = x.reshape(T, D)
    for l in range(depth):
        lam0 = 0.8 - 0.6 * math.exp(-0.3 * l)
        lamv = jnp.stack([lambda_q1[l], lambda_k1[l], lambda_q2[l], lambda_k2[l]]).astype(F32)
        proj = _inproj(h, norm_mix_g[l], w_in[l].astype(BF16), tm_proj, 1024)
        proj3 = proj.reshape(B, S, proj.shape[1])
        attn = _attention(proj3, slopes, lamv, subln_g[l], lam0, tq)
        conv = _conv(proj3, conv_w[l], conv_b[l], conv_ln_g[l], conv_ln_b[l], ts, 64)
        w_o = w_out[l].astype(BF16)
        wr, br = _router_weights(w_group_router[l], b_group_router[l], w_expert_router[l], b_expert_router[l])
        h1, logits = _outproj(h, attn.reshape(T, ATTN_WIDTH), conv.reshape(T, -1),
                              w_o[:ATTN_WIDTH], w_o[ATTN_WIDTH:], norm_ffn_g[l], wr, br, tm_out)
        ids8, gwt, cnt = _route(logits, tr)
        n_rows = T * TOP_K + N_EXPERTS * tm_moe
        starts_p, ends_p, n_pad, moe_plan = _tile_plan(cnt[:, 0].astype(jnp.int32), tm_moe, n_rows // tm_moe)
        starts_rep = jnp.broadcast_to(starts_p.astype(F32)[:, None], (N_EXPERTS, LANES))
        dest8 = _plan(ids8, starts_rep, tr)
        dest0, dest1 = dest8[0], dest8[1]
        x_sorted = _dispatch(ends_p, n_pad, dest0, dest1, h1, norm_ffn_g[l], n_rows, tr, tm_moe)
        y_sorted = _moe(moe_plan, x_sorted, w_gate[l], w_up[l], w_down[l], tm_moe)
        h = _combine(dest0, dest1, h1, gwt, norm_final_g, y_sorted, l == depth - 1, tr, tr // 2)
    return h.reshape(B, S, D)
```

```python
import functools
import math

import jax
import jax.numpy as jnp
import numpy as np
from jax import lax
from jax.experimental import pallas as pl
from jax.experimental.pallas import tpu as pltpu

N_HEADS = 8
QK_DIM = 64
HEAD_DIM = 2 * QK_DIM
ATTN_WIDTH = N_HEADS * HEAD_DIM
CONV_TAPS = 31
CONV_HALO = 32
N_GROUPS = 4
EXPERTS_PER_GROUP = 8
N_EXPERTS = N_GROUPS * EXPERTS_PER_GROUP
TOP_K = 2
EPS = 1e-6
NEG_INF = -1e30
LOG2E = 1.4426950408889634
LANES = 128
SUBLANES = 8
ROUTER_COLS = LANES
EXPERT_COL0 = SUBLANES
W_UNITS = 8
V7X_VMEM_LIMIT = 56 * 1024 * 1024

F32 = jnp.float32
BF16 = jnp.bfloat16


def _cparams(sem):
    return pltpu.CompilerParams(dimension_semantics=sem, vmem_limit_bytes=V7X_VMEM_LIMIT)


def _rmsnorm_rows(h, g):
    ms = jnp.mean(h * h, axis=-1, keepdims=True)
    return h * lax.rsqrt(ms + EPS) * g


def _inproj_body(x_ref, g_ref, w_ref, o_ref):
    xn = _rmsnorm_rows(x_ref[...], g_ref[...]).astype(BF16)
    o_ref[...] = jnp.dot(xn, w_ref[...], preferred_element_type=F32).astype(o_ref.dtype)


def _inproj(x2, g, w, tm):
    T, D = x2.shape
    N = w.shape[1]
    return pl.pallas_call(
        _inproj_body,
        grid=(T // tm,),
        in_specs=[pl.BlockSpec((tm, D), lambda i: (i, 0)),
                  pl.BlockSpec((1, D), lambda i: (0, 0)),
                  pl.BlockSpec((D, N), lambda i: (0, 0), pipeline_mode=pl.Buffered(1))],
        out_specs=pl.BlockSpec((tm, N), lambda i: (i, 0)),
        out_shape=jax.ShapeDtypeStruct((T, N), BF16),
        compiler_params=_cparams(("parallel",)),
        name="inproj",
    )(x2, g.reshape(1, D), w)


def _attn_body(qi_ref, j_ref, slopes_ref, lamv_ref, q_ref, k_ref, v_ref, g_ref, o_ref,
               qc_ref, kb_ref, m_ref, acc_ref, sa_ref, sb_ref, *, tq, nq, lam0):
    h = pl.program_id(0)
    tk = tq
    nstep = nq * (nq + 1) // 2
    slope2 = slopes_ref[h] * LOG2E
    lv = lamv_ref[...]
    lam = (jnp.exp(jnp.sum(lv[0:1] * lv[1:2], axis=1, keepdims=True))
           - jnp.exp(jnp.sum(lv[2:3] * lv[3:4], axis=1, keepdims=True)) + lam0)

    lane = lax.broadcasted_iota(jnp.int32, (tq, HEAD_DIM), 1)
    zero = jnp.zeros((tq, HEAD_DIM), BF16)
    one_cols = jnp.where(lane < 3, 1.0, 0.0).astype(BF16)

    def prepare_q(i, carry):
        rows = pl.ds(pl.multiple_of(i * tq, tq), tq)
        q = (q_ref[rows, :].astype(F32) * (QK_DIM ** -0.5 * LOG2E)).astype(BF16)
        qc_ref[i, :tq, :HEAD_DIM] = jnp.where(lane < QK_DIM, q, zero)
        qc_ref[i, tq:, :HEAD_DIM] = jnp.where(lane >= QK_DIM, q, zero)
        qc_ref[i, :tq, HEAD_DIM:] = one_cols
        qc_ref[i, tq:, HEAD_DIM:] = one_cols
        return carry

    def prepare_bias(i, carry):
        rows = pl.ds(pl.multiple_of(i * tq, tq), tq)
        bias = slope2 * (i * tq + lax.broadcasted_iota(jnp.int32, (tq, HEAD_DIM), 0)).astype(F32)
        hi = bias.astype(BF16).astype(F32)
        mid = (bias - hi).astype(BF16).astype(F32)
        lo = bias - hi - mid
        cols = jnp.where(lane == 0, hi, jnp.where(lane == 1, mid, jnp.where(lane == 2, lo, 0.0)))
        kb_ref[rows, :] = cols.astype(BF16)
        return carry

    lax.fori_loop(0, nq, prepare_q, 0)

    @pl.when(pl.program_id(1) == 0)
    def _():
        lax.fori_loop(0, nq, prepare_bias, 0)

    m_ref[...] = jnp.full(m_ref.shape, NEG_INF, F32)
    acc_ref[...] = jnp.zeros(acc_ref.shape, F32)
    ones = jnp.ones((tk, HEAD_DIM), BF16)

    def scores(s_ref, t):
        rows = pl.ds(pl.multiple_of(j_ref[t] * tk, tk), tk)
        kext = jnp.concatenate([k_ref[rows, :], kb_ref[rows, :]], axis=1)
        s_ref[...] = lax.dot_general(qc_ref[qi_ref[t]], kext, (((1,), (1,)), ((), ())),
                                     preferred_element_type=F32)

    def accumulate(s_ref, t, diagonal):
        rows = pl.ds(pl.multiple_of(j_ref[t] * tk, tk), tk)
        vext = jnp.concatenate([v_ref[rows, :], ones], axis=1)
        s = s_ref[...]
        if diagonal:
            kpos = lax.broadcasted_iota(jnp.int32, (1, tk), 1)
            qpos = lax.broadcasted_iota(jnp.int32, (tq, 1), 0)
            s = jnp.where(jnp.concatenate([qpos, qpos], axis=0) >= kpos, s, NEG_INF)
        m_old = m_ref[...]
        m_new = jnp.maximum(m_old, jnp.max(s, axis=1, keepdims=True))
        alpha = jnp.tile(jnp.exp2(m_old - m_new), (1, 2))
        p = jnp.exp2(s - jnp.tile(m_new, (1, tk // LANES))).astype(BF16)
        a = alpha * acc_ref[...] + jnp.dot(p, vext, preferred_element_type=F32)
        if diagonal:
            o = (a[:tq, :HEAD_DIM] / a[:tq, HEAD_DIM:] - lam * (a[tq:, :HEAD_DIM] / a[tq:, HEAD_DIM:]))
            ms = jnp.mean(o * o, axis=1, keepdims=True)
            o_ref[pl.ds(pl.multiple_of(qi_ref[t] * tq, tq), tq), :] = (
                o * lax.rsqrt(ms + EPS) * g_ref[...] * (1.0 - lam0)).astype(o_ref.dtype)
            m_ref[...] = jnp.full(m_ref.shape, NEG_INF, F32)
            acc_ref[...] = jnp.zeros(acc_ref.shape, F32)
        else:
            acc_ref[...] = a
            m_ref[...] = m_new

    def two_steps(t, diag0, diag1, trailing_scores=True):
        scores(sb_ref, t + 1)
        accumulate(sa_ref, t, diag0)
        if trailing_scores:
            scores(sa_ref, t + 2)
        accumulate(sb_ref, t + 1, diag1)

    def pair(i, carry):
        t = 2 * i
        d0 = j_ref[t] == qi_ref[t]
        d1 = j_ref[t + 1] == qi_ref[t + 1]

        @pl.when(d0)
        def _():
            two_steps(t, True, False)

        @pl.when(jnp.logical_and(jnp.logical_not(d0), d1))
        def _():
            two_steps(t, False, True)

        @pl.when(jnp.logical_not(jnp.logical_or(d0, d1)))
        def _():
            two_steps(t, False, False)

        return carry

    scores(sa_ref, 0)
    if nstep % 2 == 0:
        lax.fori_loop(0, nstep // 2 - 1, pair, 0)
        two_steps(nstep - 2, False, True, trailing_scores=False)
    else:
        lax.fori_loop(0, nstep // 2, pair, 0)
        accumulate(sa_ref, nstep - 1, True)


def _attention(proj3, slopes, lamv, subln_g, lam0, tq):
    B, S, _ = proj3.shape
    nq = S // tq
    steps = [(qi, j) for qi in range(nq) for j in range(qi + 1)]
    qi_tab = jnp.asarray([s[0] for s in steps], jnp.int32)
    j_tab = jnp.asarray([s[1] for s in steps], jnp.int32)
    body = functools.partial(_attn_body, tq=tq, nq=nq, lam0=lam0)
    seq = lambda col0: pl.BlockSpec((None, S, HEAD_DIM), lambda h, b, *_: (b, 0, col0 + h))
    gs = pltpu.PrefetchScalarGridSpec(
        num_scalar_prefetch=2,
        grid=(N_HEADS, B),
        in_specs=[pl.BlockSpec(memory_space=pltpu.SMEM),
                  pl.BlockSpec((4, QK_DIM), lambda h, b, *_: (0, 0)),
                  seq(0), seq(N_HEADS), seq(2 * N_HEADS),
                  pl.BlockSpec((1, HEAD_DIM), lambda h, b, *_: (0, 0))],
        out_specs=seq(0),
        scratch_shapes=[pltpu.VMEM((nq, 2 * tq, 2 * HEAD_DIM), BF16), pltpu.VMEM((S, HEAD_DIM), BF16),
                        pltpu.VMEM((2 * tq, LANES), F32), pltpu.VMEM((2 * tq, 2 * HEAD_DIM), F32),
                        pltpu.VMEM((2 * tq, tq), F32), pltpu.VMEM((2 * tq, tq), F32)],
    )
    return pl.pallas_call(
        body,
        grid_spec=gs,
        out_shape=jax.ShapeDtypeStruct((B, S, ATTN_WIDTH), BF16),
        compiler_params=_cparams(("arbitrary", "arbitrary")),
        name="diffattn",
    )(qi_tab, j_tab, slopes, lamv, proj3, proj3, proj3, subln_g.reshape(1, HEAD_DIM))


def _conv_body(a_ref, g_ref, ah_ref, gh_ref, w_ref, cb_ref, lng_ref, lnb_ref, o_ref, u_ref, c_ref,
               *, ts, rb):
    i = pl.program_id(1)
    nchunk = u_ref.shape[0]
    u = a_ref[...].astype(F32) * jax.nn.sigmoid(g_ref[...].astype(F32))
    uh = ah_ref[...].astype(F32) * jax.nn.sigmoid(gh_ref[...].astype(F32))
    uh = jnp.where(i > 0, uh, 0.0)
    for c in range(nchunk):
        u_ref[c, pl.ds(0, CONV_HALO), :] = uh[:, c * LANES:(c + 1) * LANES]
        u_ref[c, pl.ds(CONV_HALO, ts), :] = u[:, c * LANES:(c + 1) * LANES]

    first = CONV_HALO - (CONV_TAPS - 1)

    def lane_chunk(c, carry):
        for r0 in range(0, ts, rb):
            acc = jnp.zeros((rb, LANES), F32)
            for k in range(CONV_TAPS):
                acc = acc + jnp.tile(w_ref[c, k], (rb // SUBLANES, 1)) * u_ref[c, pl.ds(r0 + first + k, rb), :]
            c_ref[c, pl.ds(r0, rb), :] = acc
        return carry

    lax.fori_loop(0, nchunk, lane_chunk, 0)

    C = nchunk * LANES
    cs = [c_ref[c] + cb_ref[:, c * LANES:(c + 1) * LANES] for c in range(nchunk)]
    mean = jnp.sum(functools.reduce(jnp.add, cs), axis=1, keepdims=True) * (1.0 / C)
    xcs = [x - mean for x in cs]
    var = jnp.sum(functools.reduce(jnp.add, [x * x for x in xcs]), axis=1, keepdims=True) * (1.0 / C)
    rstd = lax.rsqrt(var + EPS)
    for c in range(nchunk):
        sl = slice(c * LANES, (c + 1) * LANES)
        y = xcs[c] * rstd * lng_ref[:, sl] + lnb_ref[:, sl]
        o_ref[:, sl] = (y * jax.nn.sigmoid(y)).astype(o_ref.dtype)


def _conv(proj3, conv_w, conv_b, ln_g, ln_b, ts, rb):
    B, S, _ = proj3.shape
    C = conv_w.shape[-1]
    nchunk = C // LANES
    a_blk = (3 * ATTN_WIDTH) // C
    hpt = ts // CONV_HALO
    body = functools.partial(_conv_body, ts=ts, rb=rb)
    w_b = jnp.broadcast_to(conv_w.reshape(CONV_TAPS, nchunk, 1, LANES).transpose(1, 0, 2, 3),
                           (nchunk, CONV_TAPS, SUBLANES, LANES))
    halo = lambda blk: (lambda b, i: (b, jnp.maximum(i * hpt - 1, 0), blk))
    vec = pl.BlockSpec((1, C), lambda b, i: (0, 0))
    return pl.pallas_call(
        body,
        grid=(B, S // ts),
        in_specs=[pl.BlockSpec((None, ts, C), lambda b, i: (b, i, a_blk)),
                  pl.BlockSpec((None, ts, C), lambda b, i: (b, i, a_blk + 1)),
                  pl.BlockSpec((None, CONV_HALO, C), halo(a_blk)),
                  pl.BlockSpec((None, CONV_HALO, C), halo(a_blk + 1)),
                  pl.BlockSpec((nchunk, CONV_TAPS, SUBLANES, LANES), lambda b, i: (0, 0, 0, 0)),
                  vec, vec, vec],
        out_specs=pl.BlockSpec((None, ts, C), lambda b, i: (b, i, 0)),
        out_shape=jax.ShapeDtypeStruct((B, S, C), BF16),
        scratch_shapes=[pltpu.VMEM((nchunk, ts + CONV_HALO, LANES), F32), pltpu.VMEM((nchunk, ts, LANES), F32)],
        compiler_params=_cparams(("parallel", "arbitrary")),
        name="convmod",
    )(proj3, proj3, proj3, proj3, w_b, conv_b.reshape(1, C), ln_g.reshape(1, C), ln_b.reshape(1, C))


def _outproj_body(x_ref, a_ref, c_ref, wa_ref, wc_ref, g_ref, wr_ref, br_ref, h_ref, lg_ref):
    acc = (jnp.dot(a_ref[...], wa_ref[...], preferred_element_type=F32)
           + jnp.dot(c_ref[...], wc_ref[...], preferred_element_type=F32))
    h = x_ref[...] + acc
    h_ref[...] = h
    xn = _rmsnorm_rows(h, g_ref[...])
    hi = xn.astype(BF16)
    lo = (xn - hi.astype(F32)).astype(BF16)
    r = (jnp.dot(hi, wr_ref[...], preferred_element_type=F32)
         + jnp.dot(lo, wr_ref[...], preferred_element_type=F32))
    lg_ref[...] = r[:, :ROUTER_COLS] + r[:, ROUTER_COLS:] + br_ref[...]


def _outproj(x2, attn2, conv2, w_attn, w_conv, g, wr, br, tm):
    T, D = x2.shape
    Ka, Kc = attn2.shape[1], conv2.shape[1]
    row = lambda n: pl.BlockSpec((tm, n), lambda i: (i, 0))
    full = lambda a, b: pl.BlockSpec((a, b), lambda i: (0, 0))
    return pl.pallas_call(
        _outproj_body,
        grid=(T // tm,),
        in_specs=[row(D), row(Ka), row(Kc), full(Ka, D), full(Kc, D), full(1, D),
                  full(D, 2 * ROUTER_COLS), full(1, ROUTER_COLS)],
        out_specs=[row(D), row(ROUTER_COLS)],
        out_shape=[jax.ShapeDtypeStruct((T, D), F32), jax.ShapeDtypeStruct((T, ROUTER_COLS), F32)],
        compiler_params=_cparams(("parallel",)),
        name="outproj",
    )(x2, attn2, conv2, w_attn, w_conv, g.reshape(1, D), wr, br)


def _route_body(lg_ref, id_ref, gw_ref, cnt_ref):
    lt = lg_ref[...].T
    tr = lt.shape[1]
    row = lax.broadcasted_iota(jnp.int32, (SUBLANES, tr), 0)
    gl = jnp.where(row < N_GROUPS, lt[0:SUBLANES], -jnp.inf)
    gmax = jnp.max(gl, axis=0, keepdims=True)
    g_sel = jnp.min(jnp.where(gl == gmax, row, SUBLANES), axis=0, keepdims=True)
    g_w = 1.0 / jnp.sum(jnp.exp(gl - gmax), axis=0, keepdims=True)
    e_in = lt[EXPERT_COL0:EXPERT_COL0 + EXPERTS_PER_GROUP]
    for g in range(1, N_GROUPS):
        lo = EXPERT_COL0 + g * EXPERTS_PER_GROUP
        e_in = jnp.where(g_sel == g, lt[lo:lo + EXPERTS_PER_GROUP], e_in)
    v1 = jnp.max(e_in, axis=0, keepdims=True)
    i1 = jnp.min(jnp.where(e_in == v1, row, SUBLANES), axis=0, keepdims=True)
    e_rest = jnp.where(row == i1, -jnp.inf, e_in)
    v2 = jnp.max(e_rest, axis=0, keepdims=True)
    i2 = jnp.min(jnp.where(e_rest == v2, row, SUBLANES), axis=0, keepdims=True)
    d = jnp.exp(v2 - v1)
    w1 = g_w / (1.0 + d)
    w2 = w1 * d
    e1 = g_sel * EXPERTS_PER_GROUP + i1
    e2 = g_sel * EXPERTS_PER_GROUP + i2
    id_ref[...] = jnp.where(row == 0, e1, jnp.where(row == 1, e2, 0))
    rowl = lax.broadcasted_iota(jnp.int32, (LANES, tr), 0)
    gw_ref[...] = jnp.where(rowl == 0, w1, jnp.where(rowl == 1, w2, 0.0)).T
    rowe = lax.broadcasted_iota(jnp.int32, (N_EXPERTS, tr), 0)
    hits = jnp.where(rowe == e1, 1.0, 0.0) + jnp.where(rowe == e2, 1.0, 0.0)

    @pl.when(pl.program_id(0) == 0)
    def _():
        cnt_ref[...] = jnp.zeros(cnt_ref.shape, F32)

    cnt_ref[...] += jnp.broadcast_to(jnp.sum(hits, axis=1, keepdims=True), cnt_ref.shape)


def _route(logits, tr):
    T = logits.shape[0]
    return pl.pallas_call(
        _route_body,
        grid=(T // tr,),
        in_specs=[pl.BlockSpec((tr, ROUTER_COLS), lambda i: (i, 0))],
        out_specs=[pl.BlockSpec((SUBLANES, tr), lambda i: (0, i)),
                   pl.BlockSpec((tr, LANES), lambda i: (i, 0)),
                   pl.BlockSpec((N_EXPERTS, LANES), lambda i: (0, 0))],
        out_shape=[jax.ShapeDtypeStruct((SUBLANES, T), jnp.int32),
                   jax.ShapeDtypeStruct((T, LANES), F32),
                   jax.ShapeDtypeStruct((N_EXPERTS, LANES), F32)],
        compiler_params=_cparams(("arbitrary",)),
        name="route",
    )(logits)


def _plan_body(id_ref, st_ref, u_ref, d_ref, base_ref):
    @pl.when(pl.program_id(0) == 0)
    def _():
        base_ref[...] = st_ref[...]

    ids = id_ref[...]
    tr = ids.shape[1]
    rowe = lax.broadcasted_iota(jnp.int32, (N_EXPERTS, tr), 0)
    base = base_ref[:, 0:1]
    dests = []
    for k in range(TOP_K):
        hit = rowe == ids[k:k + 1]
        before = jnp.dot(jnp.where(hit, 1.0, 0.0).astype(BF16), u_ref[...], preferred_element_type=F32)
        dests.append(jnp.sum(jnp.where(hit, base + before, 0.0), axis=0, keepdims=True))
        base = base + jnp.sum(jnp.where(hit, 1.0, 0.0), axis=1, keepdims=True)
    base_ref[...] = jnp.broadcast_to(base, base_ref.shape)
    row = lax.broadcasted_iota(jnp.int32, (SUBLANES, tr), 0)
    d_ref[...] = jnp.where(row == 0, dests[0], jnp.where(row == 1, dests[1], 0.0)).astype(jnp.int32)


def _plan(ids8, starts_rep, tr):
    T = ids8.shape[1]
    upper = jnp.asarray(np.triu(np.ones((tr, tr), np.float32), 1), BF16)
    return pl.pallas_call(
        _plan_body,
        grid=(T // tr,),
        in_specs=[pl.BlockSpec((SUBLANES, tr), lambda i: (0, i)),
                  pl.BlockSpec((N_EXPERTS, LANES), lambda i: (0, 0)),
                  pl.BlockSpec((tr, tr), lambda i: (0, 0))],
        out_specs=pl.BlockSpec((SUBLANES, tr), lambda i: (0, i)),
        out_shape=jax.ShapeDtypeStruct((SUBLANES, T), jnp.int32),
        scratch_shapes=[pltpu.VMEM((N_EXPERTS, LANES), F32)],
        compiler_params=_cparams(("arbitrary",)),
        name="plan",
    )(ids8, starts_rep, upper)


def _dispatch_body(endp_ref, npad_ref, d0_ref, d1_ref, h_ref, g_ref, xs_ref, xn_ref, zb_ref, sem, zsem,
                   *, tr, tm, n_steps):
    i = pl.program_id(0)
    slot = i % 2

    def wait_rows(s):
        for _ in range(TOP_K):
            pltpu.make_async_copy(xn_ref.at[s], xs_ref.at[pl.ds(0, tr), :], sem.at[s]).wait()

    @pl.when(i == 0)
    def _():
        zb_ref[...] = jnp.zeros(zb_ref.shape, F32)

        def zero_tail(e, carry):
            @pl.when(npad_ref[e] > 0)
            def _():
                tail = pl.multiple_of(endp_ref[e] - tm, tm)
                cp = pltpu.make_async_copy(zb_ref, xs_ref.at[pl.ds(tail, tm), :], zsem)
                cp.start()
                cp.wait()
            return carry

        lax.fori_loop(0, N_EXPERTS, zero_tail, 0)

        def zero_unused(j, carry):
            cp = pltpu.make_async_copy(zb_ref, xs_ref.at[pl.ds(pl.multiple_of(j * tm, tm), tm), :], zsem)
            cp.start()
            cp.wait()
            return carry

        lax.fori_loop(endp_ref[N_EXPERTS - 1] // tm, xs_ref.shape[0] // tm, zero_unused, 0)

    @pl.when(i >= 2)
    def _():
        wait_rows(slot)

    xn_ref[slot] = _rmsnorm_rows(h_ref[...], g_ref[...])

    def scatter(t, carry):
        src = xn_ref.at[slot, pl.ds(t, 1), :]
        pltpu.make_async_copy(src, xs_ref.at[pl.ds(d0_ref[t], 1), :], sem.at[slot]).start(priority=0)
        pltpu.make_async_copy(src, xs_ref.at[pl.ds(d1_ref[t], 1), :], sem.at[slot]).start(priority=1)
        return carry

    lax.fori_loop(0, tr, scatter, 0, unroll=8)

    @pl.when(i == n_steps - 1)
    def _():
        wait_rows(slot)
        if n_steps > 1:
            wait_rows(1 - slot)


def _dispatch(ends_p, n_pad, dest0, dest1, h1, g, n_rows, tr, tm):
    T, D = h1.shape
    n_steps = T // tr
    gs = pltpu.PrefetchScalarGridSpec(
        num_scalar_prefetch=2,
        grid=(n_steps,),
        in_specs=[pl.BlockSpec((tr,), lambda i, ep, npd: (i,), memory_space=pltpu.SMEM),
                  pl.BlockSpec((tr,), lambda i, ep, npd: (i,), memory_space=pltpu.SMEM),
                  pl.BlockSpec((tr, D), lambda i, ep, npd: (i, 0)),
                  pl.BlockSpec((1, D), lambda i, ep, npd: (0, 0))],
        out_specs=pl.BlockSpec(memory_space=pl.ANY),
        scratch_shapes=[pltpu.VMEM((2, tr, D), F32), pltpu.VMEM((tm, D), F32),
                        pltpu.SemaphoreType.DMA((2,)), pltpu.SemaphoreType.DMA(())],
    )
    return pl.pallas_call(
        functools.partial(_dispatch_body, tr=tr, tm=tm, n_steps=n_steps),
        grid_spec=gs,
        out_shape=jax.ShapeDtypeStruct((n_rows, D), F32),
        compiler_params=_cparams(("arbitrary",)),
        name="dispatch",
    )(ends_p, n_pad, dest0, dest1, h1, g.reshape(1, D))


def _moe_body(te_ref, nu_ref, ne_ref, cb_ref, ce_ref, sl_ref,
              x_ref, wg_hbm, wu_hbm, wd_hbm, y_ref,
              wg_s, wu_s, wd_s, sg, su, sd, sem):
    i = pl.program_id(0)
    slot = sl_ref[i]
    fu = wg_s.shape[-1] // W_UNITS

    def unit_copies(e, c, par):
        col = pl.multiple_of(c * fu, fu)
        return (pltpu.make_async_copy(wg_hbm.at[e, :, pl.ds(col, fu)], sg.at[par], sem.at[par]),
                pltpu.make_async_copy(wu_hbm.at[e, :, pl.ds(col, fu)], su.at[par], sem.at[par]),
                pltpu.make_async_copy(wd_hbm.at[e, pl.ds(col, fu), :], sd.at[par], sem.at[par]))

    def start_unit(e, c, par):
        for cp in unit_copies(e, c, par):
            cp.start()

    def wait_unit(e, c, par):
        for cp in unit_copies(e, c, par):
            cp.wait()

    def convert_unit(c, par, dst_slot):
        col = pl.multiple_of(c * fu, fu)
        wg_s[dst_slot, :, pl.ds(col, fu)] = sg[par].astype(BF16)
        wu_s[dst_slot, :, pl.ds(col, fu)] = su[par].astype(BF16)
        wd_s[dst_slot, pl.ds(col, fu), :] = sd[par].astype(BF16)

    def load_units(e, c_lo, c_hi, dst_slot):
        def unit(c, carry):
            par = c % 2
            wait_unit(e, c, par)

            @pl.when(c + 1 < W_UNITS)
            def _():
                start_unit(e, c + 1, 1 - par)

            convert_unit(c, par, dst_slot)
            return carry

        lax.fori_loop(c_lo, c_hi, unit, 0)

    @pl.when(i == 0)
    def _():
        start_unit(te_ref[0], 0, 0)
        load_units(te_ref[0], 0, W_UNITS, slot)

    @pl.when(jnp.logical_and(cb_ref[i] == 0, ce_ref[i] > 0))
    def _():
        start_unit(ne_ref[i], 0, 0)

    @pl.when(i < nu_ref[0])
    def _():
        x = x_ref[...].astype(BF16)
        a = jnp.dot(x, wg_s[slot], preferred_element_type=F32)
        b = jnp.dot(x, wu_s[slot], preferred_element_type=F32)
        hid = (a * jax.nn.sigmoid(a) * b).astype(BF16)
        y_ref[...] = jnp.dot(hid, wd_s[slot], preferred_element_type=F32)

    @pl.when(i >= nu_ref[0])
    def _():
        y_ref[...] = jnp.zeros(y_ref.shape, y_ref.dtype)

    load_units(ne_ref[i], cb_ref[i], ce_ref[i], 1 - slot)


def _moe(plan, x_sorted, w_gate, w_up, w_down, tm):
    P, D = x_sorted.shape
    F = w_gate.shape[-1]
    fu = F // W_UNITS
    n_pref = len(plan)
    x_map = lambda i, te, nu, *_: (jnp.minimum(i, nu[0] - 1), 0)
    hbm = pl.BlockSpec(memory_space=pl.ANY)
    gs = pltpu.PrefetchScalarGridSpec(
        num_scalar_prefetch=n_pref,
        grid=(P // tm,),
        in_specs=[pl.BlockSpec((tm, D), x_map), hbm, hbm, hbm],
        out_specs=pl.BlockSpec((tm, D), lambda i, *_: (i, 0)),
        scratch_shapes=[pltpu.VMEM((2, D, F), BF16), pltpu.VMEM((2, D, F), BF16), pltpu.VMEM((2, F, D), BF16),
                        pltpu.VMEM((2, D, fu), F32), pltpu.VMEM((2, D, fu), F32), pltpu.VMEM((2, fu, D), F32),
                        pltpu.SemaphoreType.DMA((2,))],
    )
    return pl.pallas_call(
        _moe_body,
        grid_spec=gs,
        out_shape=jax.ShapeDtypeStruct((P, D), F32),
        compiler_params=_cparams(("arbitrary",)),
        name="moe",
    )(*plan, x_sorted, w_gate, w_up, w_down)


def _combine_body(d0_ref, d1_ref, h_ref, gw_ref, g_ref, ys_hbm, o_ref, yb_ref, sem, *, tr, th, final_norm):
    halves = tr // th

    def gather(hf):
        def one(t, carry):
            tok = hf * th + t
            pltpu.make_async_copy(ys_hbm.at[pl.ds(d0_ref[tok], 1), :], yb_ref.at[hf, 0, pl.ds(t, 1), :],
                                  sem.at[hf]).start(priority=0)
            pltpu.make_async_copy(ys_hbm.at[pl.ds(d1_ref[tok], 1), :], yb_ref.at[hf, 1, pl.ds(t, 1), :],
                                  sem.at[hf]).start(priority=1)
            return carry

        lax.fori_loop(0, th, one, 0, unroll=8)

    for hf in range(halves):
        gather(hf)
    for hf in range(halves):
        for k in range(TOP_K):
            pltpu.make_async_copy(ys_hbm.at[pl.ds(0, th), :], yb_ref.at[hf, k], sem.at[hf]).wait()
        rows = pl.ds(hf * th, th)
        gw = gw_ref[rows, :]
        h = h_ref[rows, :] + gw[:, 0:1] * yb_ref[hf, 0] + gw[:, 1:2] * yb_ref[hf, 1]
        if final_norm:
            h = _rmsnorm_rows(h, g_ref[...])
        o_ref[rows, :] = h


def _combine(dest0, dest1, h1, gwt, g, y_sorted, final_norm, tr, th):
    T, D = h1.shape
    smem = pl.BlockSpec((tr,), lambda i: (i,), memory_space=pltpu.SMEM)
    row = pl.BlockSpec((tr, D), lambda i: (i, 0))
    return pl.pallas_call(
        functools.partial(_combine_body, tr=tr, th=th, final_norm=final_norm),
        grid=(T // tr,),
        in_specs=[smem, smem, row, pl.BlockSpec((tr, LANES), lambda i: (i, 0)),
                  pl.BlockSpec((1, D), lambda i: (0, 0)), pl.BlockSpec(memory_space=pl.ANY)],
        out_specs=row,
        out_shape=jax.ShapeDtypeStruct((T, D), F32),
        scratch_shapes=[pltpu.VMEM((tr // th, TOP_K, th, D), F32), pltpu.SemaphoreType.DMA((tr // th,))],
        compiler_params=_cparams(("arbitrary",)),
        name="combine",
    )(dest0, dest1, h1, gwt, g.reshape(1, D), y_sorted)


def _router_weights(w_group, b_group, w_expert, b_expert):
    def pack(g, e):
        gap = jnp.zeros(g.shape[:-1] + (EXPERT_COL0 - N_GROUPS,), F32)
        tail = jnp.zeros(g.shape[:-1] + (ROUTER_COLS - EXPERT_COL0 - N_EXPERTS,), F32)
        return jnp.concatenate([g.astype(F32), gap, e.astype(F32), tail], axis=-1)

    w = pack(w_group, w_expert)
    b = pack(b_group[None, :], b_expert[None, :])
    w_hi = w.astype(BF16)
    w_lo = (w - w_hi.astype(F32)).astype(BF16)
    return jnp.concatenate([w_hi, w_lo], axis=1), b


def _tile_plan(counts, tm, n_tiles):
    i32 = jnp.int32
    eidx = jnp.arange(N_EXPERTS, dtype=i32)
    tiles_e = (counts + tm - 1) // tm
    n_pad = tiles_e * tm
    ends_p = jnp.cumsum(n_pad).astype(i32)
    starts_p = ends_p - n_pad
    cum_tiles = jnp.cumsum(tiles_e).astype(i32)
    n_used = cum_tiles[-1]
    used = tiles_e > 0
    last_e = jnp.max(jnp.where(used, eidx, 0))
    ti = jnp.arange(n_tiles, dtype=i32)
    te = jnp.minimum(jnp.sum((ti[:, None] >= cum_tiles[None, :]).astype(i32), axis=1), last_e)
    t_in = ti - (cum_tiles - tiles_e)[te]
    t_n = jnp.maximum(tiles_e[te], 1)
    nxt = jnp.min(jnp.where(used[None, :] & (eidx[None, :] > eidx[:, None]), eidx[None, :], N_EXPERTS), axis=1)
    has_next = (nxt < N_EXPERTS)[te] & (ti < n_used)
    ne = jnp.where(has_next, nxt[te], 0).astype(i32)
    cb = jnp.where(has_next, (W_UNITS * t_in) // t_n, 0).astype(i32)
    ce = jnp.where(has_next, (W_UNITS * (t_in + 1)) // t_n, 0).astype(i32)
    slot = ((jnp.cumsum(used.astype(i32)) - 1)[te] % 2).astype(i32)
    moe_plan = (te.astype(i32), n_used.reshape(1).astype(i32), ne, cb, ce, slot)
    return starts_p, ends_p, n_pad.astype(i32), moe_plan


def kernel(x, norm_mix_g, w_in, lambda_q1, lambda_k1, lambda_q2, lambda_k2, subln_g, conv_w, conv_b,
           conv_ln_g, conv_ln_b, w_out, norm_ffn_g, w_group_router, b_group_router, w_expert_router,
           b_expert_router, w_gate, w_up, w_down, norm_final_g):
    B, S, D = x.shape
    T = B * S
    depth = w_in.shape[0]
    tm_proj = min(512, T)
    tm_out = min(512, T)
    tq = min(512, S)
    ts = min(512, S)
    tm_moe = 256
    tr = min(1024, T)
    slopes = 2.0 ** (-8.0 * (jnp.arange(N_HEADS, dtype=F32) + 1.0) / N_HEADS)

    h = x.reshape(T, D)
    for l in range(depth):
        lam0 = 0.8 - 0.6 * math.exp(-0.3 * l)
        lamv = jnp.stack([lambda_q1[l], lambda_k1[l], lambda_q2[l], lambda_k2[l]]).astype(F32)
        proj = _inproj(h, norm_mix_g[l], w_in[l].astype(BF16), tm_proj)
        proj3 = proj.reshape(B, S, proj.shape[1])
        attn = _attention(proj3, slopes, lamv, subln_g[l], lam0, tq)
        conv = _conv(proj3, conv_w[l], conv_b[l], conv_ln_g[l], conv_ln_b[l], ts, 64)
        w_o = w_out[l].astype(BF16)
        wr, br = _router_weights(w_group_router[l], b_group_router[l], w_expert_router[l], b_expert_router[l])
        h1, logits = _outproj(h, attn.reshape(T, ATTN_WIDTH), conv.reshape(T, -1),
                              w_o[:ATTN_WIDTH], w_o[ATTN_WIDTH:], norm_ffn_g[l], wr, br, tm_out)
        ids8, gwt, cnt = _route(logits, tr)
        n_rows = T * TOP_K + N_EXPERTS * tm_moe
        starts_p, ends_p, n_pad, moe_plan = _tile_plan(cnt[:, 0].astype(jnp.int32), tm_moe, n_rows // tm_moe)
        starts_rep = jnp.broadcast_to(starts_p.astype(F32)[:, None], (N_EXPERTS, LANES))
        dest8 = _plan(ids8, starts_rep, tr)
        dest0, dest1 = dest8[0], dest8[1]
        x_sorted = _dispatch(ends_p, n_pad, dest0, dest1, h1, norm_ffn_g[l], n_rows, tr, tm_moe)
        y_sorted = _moe(moe_plan, x_sorted, w_gate[l], w_up[l], w_down[l], tm_moe)
        h = _combine(dest0, dest1, h1, gwt, norm_final_g, y_sorted, l == depth - 1, tr, tr // 2)
    return h.reshape(B, S, D)
```

```python
import functools
import math

import jax
import jax.numpy as jnp
import numpy as np
from jax import lax
from jax.experimental import pallas as pl
from jax.experimental.pallas import tpu as pltpu

N_HEADS = 8
QK_DIM = 64
HEAD_DIM = 2 * QK_DIM
ATTN_WIDTH = N_HEADS * HEAD_DIM
CONV_TAPS = 31
CONV_HALO = 32
N_GROUPS = 4
EXPERTS_PER_GROUP = 8
N_EXPERTS = N_GROUPS * EXPERTS_PER_GROUP
TOP_K = 2
EPS = 1e-6
NEG_INF = -1e30
LOG2E = 1.4426950408889634
LANES = 128
SUBLANES = 8
ROUTER_COLS = LANES
EXPERT_COL0 = SUBLANES
W_UNITS = 8
V7X_VMEM_LIMIT = 56 * 1024 * 1024

F32 = jnp.float32
BF16 = jnp.bfloat16


def _cparams(sem):
    return pltpu.CompilerParams(dimension_semantics=sem, vmem_limit_bytes=V7X_VMEM_LIMIT)


def _rmsnorm_rows(h, g):
    ms = jnp.mean(h * h, axis=-1, keepdims=True)
    return h * lax.rsqrt(ms + EPS) * g


def _inproj_body(x_ref, g_ref, w_ref, o_ref):
    xn = _rmsnorm_rows(x_ref[...], g_ref[...]).astype(BF16)
    o_ref[...] = jnp.dot(xn, w_ref[...], preferred_element_type=F32).astype(o_ref.dtype)


def _inproj(x2, g, w, tm):
    T, D = x2.shape
    N = w.shape[1]
    return pl.pallas_call(
        _inproj_body,
        grid=(T // tm,),
        in_specs=[pl.BlockSpec((tm, D), lambda i: (i, 0)),
                  pl.BlockSpec((1, D), lambda i: (0, 0)),
                  pl.BlockSpec((D, N), lambda i: (0, 0), pipeline_mode=pl.Buffered(1))],
        out_specs=pl.BlockSpec((tm, N), lambda i: (i, 0)),
        out_shape=jax.ShapeDtypeStruct((T, N), BF16),
        compiler_params=_cparams(("parallel",)),
        name="inproj",
    )(x2, g.reshape(1, D), w)


def _attn_body(qi_ref, j_ref, slopes_ref, lamv_ref, q_ref, k_ref, v_ref, g_ref, o_ref,
               qc_ref, kb_ref, m_ref, acc_ref, sa_ref, sb_ref, *, tq, nq, lam0):
    h = pl.program_id(0)
    tk = tq
    nstep = nq * (nq + 1) // 2
    slope2 = slopes_ref[h] * LOG2E
    lv = lamv_ref[...]
    lam = (jnp.exp(jnp.sum(lv[0:1] * lv[1:2], axis=1, keepdims=True))
           - jnp.exp(jnp.sum(lv[2:3] * lv[3:4], axis=1, keepdims=True)) + lam0)

    lane = lax.broadcasted_iota(jnp.int32, (tq, HEAD_DIM), 1)
    zero = jnp.zeros((tq, HEAD_DIM), BF16)
    one_cols = jnp.where(lane < 3, 1.0, 0.0).astype(BF16)

    def prepare_q(i, carry):
        rows = pl.ds(pl.multiple_of(i * tq, tq), tq)
        q = (q_ref[rows, :].astype(F32) * (QK_DIM ** -0.5 * LOG2E)).astype(BF16)
        qc_ref[i, :tq, :HEAD_DIM] = jnp.where(lane < QK_DIM, q, zero)
        qc_ref[i, tq:, :HEAD_DIM] = jnp.where(lane >= QK_DIM, q, zero)
        qc_ref[i, :tq, HEAD_DIM:] = one_cols
        qc_ref[i, tq:, HEAD_DIM:] = one_cols
        return carry

    def prepare_bias(i, carry):
        rows = pl.ds(pl.multiple_of(i * tq, tq), tq)
        bias = slope2 * (i * tq + lax.broadcasted_iota(jnp.int32, (tq, HEAD_DIM), 0)).astype(F32)
        hi = bias.astype(BF16).astype(F32)
        mid = (bias - hi).astype(BF16).astype(F32)
        lo = bias - hi - mid
        cols = jnp.where(lane == 0, hi, jnp.where(lane == 1, mid, jnp.where(lane == 2, lo, 0.0)))
        kb_ref[rows, :] = cols.astype(BF16)
        return carry

    lax.fori_loop(0, nq, prepare_q, 0)

    @pl.when(pl.program_id(1) == 0)
    def _():
        lax.fori_loop(0, nq, prepare_bias, 0)

    m_ref[...] = jnp.full(m_ref.shape, NEG_INF, F32)
    acc_ref[...] = jnp.zeros(acc_ref.shape, F32)
    ones = jnp.ones((tk, HEAD_DIM), BF16)

    def scores(s_ref, t):
        rows = pl.ds(pl.multiple_of(j_ref[t] * tk, tk), tk)
        kext = jnp.concatenate([k_ref[rows, :], kb_ref[rows, :]], axis=1)
        s_ref[...] = lax.dot_general(qc_ref[qi_ref[t]], kext, (((1,), (1,)), ((), ())),
                                     preferred_element_type=F32)

    def accumulate(s_ref, t, diagonal):
        rows = pl.ds(pl.multiple_of(j_ref[t] * tk, tk), tk)
        vext = jnp.concatenate([v_ref[rows, :], ones], axis=1)
        s = s_ref[...]
        if diagonal:
            kpos = lax.broadcasted_iota(jnp.int32, (1, tk), 1)
            qpos = lax.broadcasted_iota(jnp.int32, (tq, 1), 0)
            s = jnp.where(jnp.concatenate([qpos, qpos], axis=0) >= kpos, s, NEG_INF)
        m_old = m_ref[...]
        m_new = jnp.maximum(m_old, jnp.max(s, axis=1, keepdims=True))
        alpha = jnp.tile(jnp.exp2(m_old - m_new), (1, 2))
        p = jnp.exp2(s - jnp.tile(m_new, (1, tk // LANES))).astype(BF16)
        a = alpha * acc_ref[...] + jnp.dot(p, vext, preferred_element_type=F32)
        if diagonal:
            o = (a[:tq, :HEAD_DIM] / a[:tq, HEAD_DIM:] - lam * (a[tq:, :HEAD_DIM] / a[tq:, HEAD_DIM:]))
            ms = jnp.mean(o * o, axis=1, keepdims=True)
            o_ref[pl.ds(pl.multiple_of(qi_ref[t] * tq, tq), tq), :] = (
                o * lax.rsqrt(ms + EPS) * g_ref[...] * (1.0 - lam0)).astype(o_ref.dtype)
            m_ref[...] = jnp.full(m_ref.shape, NEG_INF, F32)
            acc_ref[...] = jnp.zeros(acc_ref.shape, F32)
        else:
            acc_ref[...] = a
            m_ref[...] = m_new

    def two_steps(t, diag0, diag1, trailing_scores=True):
        scores(sb_ref, t + 1)
        accumulate(sa_ref, t, diag0)
        if trailing_scores:
            scores(sa_ref, t + 2)
        accumulate(sb_ref, t + 1, diag1)

    def pair(i, carry):
        t = 2 * i
        d0 = j_ref[t] == qi_ref[t]
        d1 = j_ref[t + 1] == qi_ref[t + 1]

        @pl.when(d0)
        def _():
            two_steps(t, True, False)

        @pl.when(jnp.logical_and(jnp.logical_not(d0), d1))
        def _():
            two_steps(t, False, True)

        @pl.when(jnp.logical_not(jnp.logical_or(d0, d1)))
        def _():
            two_steps(t, False, False)

        return carry

    scores(sa_ref, 0)
    if nstep % 2 == 0:
        lax.fori_loop(0, nstep // 2 - 1, pair, 0)
        two_steps(nstep - 2, False, True, trailing_scores=False)
    else:
        lax.fori_loop(0, nstep // 2, pair, 0)
        accumulate(sa_ref, nstep - 1, True)


def _attention(proj3, slopes, lamv, subln_g, lam0, tq):
    B, S, _ = proj3.shape
    nq = S // tq
    steps = [(qi, j) for qi in range(nq) for j in range(qi + 1)]
    qi_tab = jnp.asarray([s[0] for s in steps], jnp.int32)
    j_tab = jnp.asarray([s[1] for s in steps], jnp.int32)
    body = functools.partial(_attn_body, tq=tq, nq=nq, lam0=lam0)
    seq = lambda col0: pl.BlockSpec((None, S, HEAD_DIM), lambda h, b, *_: (b, 0, col0 + h))
    gs = pltpu.PrefetchScalarGridSpec(
        num_scalar_prefetch=2,
        grid=(N_HEADS, B),
        in_specs=[pl.BlockSpec(memory_space=pltpu.SMEM),
                  pl.BlockSpec((4, QK_DIM), lambda h, b, *_: (0, 0)),
                  seq(0), seq(N_HEADS), seq(2 * N_HEADS),
                  pl.BlockSpec((1, HEAD_DIM), lambda h, b, *_: (0, 0))],
        out_specs=seq(0),
        scratch_shapes=[pltpu.VMEM((nq, 2 * tq, 2 * HEAD_DIM), BF16), pltpu.VMEM((S, HEAD_DIM), BF16),
                        pltpu.VMEM((2 * tq, LANES), F32), pltpu.VMEM((2 * tq, 2 * HEAD_DIM), F32),
                        pltpu.VMEM((2 * tq, tq), F32), pltpu.VMEM((2 * tq, tq), F32)],
    )
    return pl.pallas_call(
        body,
        grid_spec=gs,
        out_shape=jax.ShapeDtypeStruct((B, S, ATTN_WIDTH), BF16),
        compiler_params=_cparams(("arbitrary", "arbitrary")),
        name="diffattn",
    )(qi_tab, j_tab, slopes, lamv, proj3, proj3, proj3, subln_g.reshape(1, HEAD_DIM))


def _conv_body(a_ref, g_ref, ah_ref, gh_ref, w_ref, cb_ref, lng_ref, lnb_ref, o_ref, u_ref, c_ref,
               *, ts, rb):
    i = pl.program_id(1)
    nchunk = u_ref.shape[0]
    u = a_ref[...].astype(F32) * jax.nn.sigmoid(g_ref[...].astype(F32))
    uh = ah_ref[...].astype(F32) * jax.nn.sigmoid(gh_ref[...].astype(F32))
    uh = jnp.where(i > 0, uh, 0.0)
    for c in range(nchunk):
        u_ref[c, pl.ds(0, CONV_HALO), :] = uh[:, c * LANES:(c + 1) * LANES]
        u_ref[c, pl.ds(CONV_HALO, ts), :] = u[:, c * LANES:(c + 1) * LANES]

    first = CONV_HALO - (CONV_TAPS - 1)

    def lane_chunk(c, carry):
        for r0 in range(0, ts, rb):
            acc = jnp.zeros((rb, LANES), F32)
            for k in range(CONV_TAPS):
                acc = acc + jnp.tile(w_ref[c, k], (rb // SUBLANES, 1)) * u_ref[c, pl.ds(r0 + first + k, rb), :]
            c_ref[c, pl.ds(r0, rb), :] = acc
        return carry

    lax.fori_loop(0, nchunk, lane_chunk, 0)

    C = nchunk * LANES
    cs = [c_ref[c] + cb_ref[:, c * LANES:(c + 1) * LANES] for c in range(nchunk)]
    mean = jnp.sum(functools.reduce(jnp.add, cs), axis=1, keepdims=True) * (1.0 / C)
    xcs = [x - mean for x in cs]
    var = jnp.sum(functools.reduce(jnp.add, [x * x for x in xcs]), axis=1, keepdims=True) * (1.0 / C)
    rstd = lax.rsqrt(var + EPS)
    for c in range(nchunk):
        sl = slice(c * LANES, (c + 1) * LANES)
        y = xcs[c] * rstd * lng_ref[:, sl] + lnb_ref[:, sl]
        o_ref[:, sl] = (y * jax.nn.sigmoid(y)).astype(o_ref.dtype)


def _conv(proj3, conv_w, conv_b, ln_g, ln_b, ts, rb):
    B, S, _ = proj3.shape
    C = conv_w.shape[-1]
    nchunk = C // LANES
    a_blk = (3 * ATTN_WIDTH) // C
    hpt = ts // CONV_HALO
    body = functools.partial(_conv_body, ts=ts, rb=rb)
    w_b = jnp.broadcast_to(conv_w.reshape(CONV_TAPS, nchunk, 1, LANES).transpose(1, 0, 2, 3),
                           (nchunk, CONV_TAPS, SUBLANES, LANES))
    halo = lambda blk: (lambda b, i: (b, jnp.maximum(i * hpt - 1, 0), blk))
    vec = pl.BlockSpec((1, C), lambda b, i: (0, 0))
    return pl.pallas_call(
        body,
        grid=(B, S // ts),
        in_specs=[pl.BlockSpec((None, ts, C), lambda b, i: (b, i, a_blk)),
                  pl.BlockSpec((None, ts, C), lambda b, i: (b, i, a_blk + 1)),
                  pl.BlockSpec((None, CONV_HALO, C), halo(a_blk)),
                  pl.BlockSpec((None, CONV_HALO, C), halo(a_blk + 1)),
                  pl.BlockSpec((nchunk, CONV_TAPS, SUBLANES, LANES), lambda b, i: (0, 0, 0, 0)),
                  vec, vec, vec],
        out_specs=pl.BlockSpec((None, ts, C), lambda b, i: (b, i, 0)),
        out_shape=jax.ShapeDtypeStruct((B, S, C), BF16),
        scratch_shapes=[pltpu.VMEM((nchunk, ts + CONV_HALO, LANES), F32), pltpu.VMEM((nchunk, ts, LANES), F32)],
        compiler_params=_cparams(("parallel", "arbitrary")),
        name="convmod",
    )(proj3, proj3, proj3, proj3, w_b, conv_b.reshape(1, C), ln_g.reshape(1, C), ln_b.reshape(1, C))


def _outproj_body(x_ref, a_ref, c_ref, wa_ref, wc_ref, g_ref, wr_ref, br_ref, h_ref, lg_ref, *, parts):
    rp = x_ref.shape[0] // parts
    for s in range(parts):
        rows = slice(s * rp, (s + 1) * rp)
        acc = (jnp.dot(a_ref[rows, :], wa_ref[...], preferred_element_type=F32)
               + jnp.dot(c_ref[rows, :], wc_ref[...], preferred_element_type=F32))
        h = x_ref[rows, :] + acc
        h_ref[rows, :] = h
        xn = _rmsnorm_rows(h, g_ref[...])
        hi = xn.astype(BF16)
        lo = (xn - hi.astype(F32)).astype(BF16)
        r = (jnp.dot(hi, wr_ref[...], preferred_element_type=F32)
             + jnp.dot(lo, wr_ref[...], preferred_element_type=F32))
        lg_ref[rows, :] = r[:, :ROUTER_COLS] + r[:, ROUTER_COLS:] + br_ref[...]


def _outproj(x2, attn2, conv2, w_attn, w_conv, g, wr, br, tm):
    T, D = x2.shape
    Ka, Kc = attn2.shape[1], conv2.shape[1]
    row = lambda n: pl.BlockSpec((tm, n), lambda i: (i, 0))
    full = lambda a, b: pl.BlockSpec((a, b), lambda i: (0, 0))
    return pl.pallas_call(
        functools.partial(_outproj_body, parts=4 if tm % 32 == 0 else 1),
        grid=(T // tm,),
        in_specs=[row(D), row(Ka), row(Kc), full(Ka, D), full(Kc, D), full(1, D),
                  full(D, 2 * ROUTER_COLS), full(1, ROUTER_COLS)],
        out_specs=[row(D), row(ROUTER_COLS)],
        out_shape=[jax.ShapeDtypeStruct((T, D), F32), jax.ShapeDtypeStruct((T, ROUTER_COLS), F32)],
        compiler_params=_cparams(("parallel",)),
        name="outproj",
    )(x2, attn2, conv2, w_attn, w_conv, g.reshape(1, D), wr, br)


def _route_body(lg_ref, id_ref, gw_ref, cnt_ref):
    lt = lg_ref[...].T
    tr = lt.shape[1]
    row = lax.broadcasted_iota(jnp.int32, (SUBLANES, tr), 0)
    gl = jnp.where(row < N_GROUPS, lt[0:SUBLANES], -jnp.inf)
    gmax = jnp.max(gl, axis=0, keepdims=True)
    g_sel = jnp.min(jnp.where(gl == gmax, row, SUBLANES), axis=0, keepdims=True)
    g_w = 1.0 / jnp.sum(jnp.exp(gl - gmax), axis=0, keepdims=True)
    e_in = lt[EXPERT_COL0:EXPERT_COL0 + EXPERTS_PER_GROUP]
    for g in range(1, N_GROUPS):
        lo = EXPERT_COL0 + g * EXPERTS_PER_GROUP
        e_in = jnp.where(g_sel == g, lt[lo:lo + EXPERTS_PER_GROUP], e_in)
    v1 = jnp.max(e_in, axis=0, keepdims=True)
    i1 = jnp.min(jnp.where(e_in == v1, row, SUBLANES), axis=0, keepdims=True)
    e_rest = jnp.where(row == i1, -jnp.inf, e_in)
    v2 = jnp.max(e_rest, axis=0, keepdims=True)
    i2 = jnp.min(jnp.where(e_rest == v2, row, SUBLANES), axis=0, keepdims=True)
    d = jnp.exp(v2 - v1)
    w1 = g_w / (1.0 + d)
    w2 = w1 * d
    e1 = g_sel * EXPERTS_PER_GROUP + i1
    e2 = g_sel * EXPERTS_PER_GROUP + i2
    id_ref[...] = jnp.where(row == 0, e1, jnp.where(row == 1, e2, 0))
    rowl = lax.broadcasted_iota(jnp.int32, (LANES, tr), 0)
    gw_ref[...] = jnp.where(rowl == 0, w1, jnp.where(rowl == 1, w2, 0.0)).T
    rowe = lax.broadcasted_iota(jnp.int32, (N_EXPERTS, tr), 0)
    hits = jnp.where(rowe == e1, 1.0, 0.0) + jnp.where(rowe == e2, 1.0, 0.0)

    @pl.when(pl.program_id(0) == 0)
    def _():
        cnt_ref[...] = jnp.zeros(cnt_ref.shape, F32)

    cnt_ref[...] += jnp.broadcast_to(jnp.sum(hits, axis=1, keepdims=True), cnt_ref.shape)


def _route(logits, tr):
    T = logits.shape[0]
    return pl.pallas_call(
        _route_body,
        grid=(T // tr,),
        in_specs=[pl.BlockSpec((tr, ROUTER_COLS), lambda i: (i, 0))],
        out_specs=[pl.BlockSpec((SUBLANES, tr), lambda i: (0, i)),
                   pl.BlockSpec((tr, LANES), lambda i: (i, 0)),
                   pl.BlockSpec((N_EXPERTS, LANES), lambda i: (0, 0))],
        out_shape=[jax.ShapeDtypeStruct((SUBLANES, T), jnp.int32),
                   jax.ShapeDtypeStruct((T, LANES), F32),
                   jax.ShapeDtypeStruct((N_EXPERTS, LANES), F32)],
        compiler_params=_cparams(("arbitrary",)),
        name="route",
    )(logits)


def _plan_body(id_ref, st_ref, u_ref, d_ref, base_ref):
    @pl.when(pl.program_id(0) == 0)
    def _():
        base_ref[...] = st_ref[...]

    ids = id_ref[...]
    tr = ids.shape[1]
    rowe = lax.broadcasted_iota(jnp.int32, (N_EXPERTS, tr), 0)
    base = base_ref[:, 0:1]
    dests = []
    for k in range(TOP_K):
        hit = rowe == ids[k:k + 1]
        before = jnp.dot(jnp.where(hit, 1.0, 0.0).astype(BF16), u_ref[...], preferred_element_type=F32)
        dests.append(jnp.sum(jnp.where(hit, base + before, 0.0), axis=0, keepdims=True))
        base = base + jnp.sum(jnp.where(hit, 1.0, 0.0), axis=1, keepdims=True)
    base_ref[...] = jnp.broadcast_to(base, base_ref.shape)
    row = lax.broadcasted_iota(jnp.int32, (SUBLANES, tr), 0)
    d_ref[...] = jnp.where(row == 0, dests[0], jnp.where(row == 1, dests[1], 0.0)).astype(jnp.int32)


def _plan(ids8, starts_rep, tr):
    T = ids8.shape[1]
    upper = jnp.asarray(np.triu(np.ones((tr, tr), np.float32), 1), BF16)
    return pl.pallas_call(
        _plan_body,
        grid=(T // tr,),
        in_specs=[pl.BlockSpec((SUBLANES, tr), lambda i: (0, i)),
                  pl.BlockSpec((N_EXPERTS, LANES), lambda i: (0, 0)),
                  pl.BlockSpec((tr, tr), lambda i: (0, 0))],
        out_specs=pl.BlockSpec((SUBLANES, tr), lambda i: (0, i)),
        out_shape=jax.ShapeDtypeStruct((SUBLANES, T), jnp.int32),
        scratch_shapes=[pltpu.VMEM((N_EXPERTS, LANES), F32)],
        compiler_params=_cparams(("arbitrary",)),
        name="plan",
    )(ids8, starts_rep, upper)


def _dispatch_body(endp_ref, npad_ref, d0_ref, d1_ref, h_ref, g_ref, xs_ref, xn_ref, zb_ref, sem, zsem,
                   *, tr, tm, n_steps):
    i = pl.program_id(0)
    slot = i % 2

    def wait_rows(s):
        for _ in range(TOP_K):
            pltpu.make_async_copy(xn_ref.at[s], xs_ref.at[pl.ds(0, tr), :], sem.at[s]).wait()

    @pl.when(i == 0)
    def _():
        zb_ref[...] = jnp.zeros(zb_ref.shape, F32)

        def zero_tail(e, carry):
            @pl.when(npad_ref[e] > 0)
            def _():
                tail = pl.multiple_of(endp_ref[e] - tm, tm)
                cp = pltpu.make_async_copy(zb_ref, xs_ref.at[pl.ds(tail, tm), :], zsem)
                cp.start()
                cp.wait()
            return carry

        lax.fori_loop(0, N_EXPERTS, zero_tail, 0)

        def zero_unused(j, carry):
            cp = pltpu.make_async_copy(zb_ref, xs_ref.at[pl.ds(pl.multiple_of(j * tm, tm), tm), :], zsem)
            cp.start()
            cp.wait()
            return carry

        lax.fori_loop(endp_ref[N_EXPERTS - 1] // tm, xs_ref.shape[0] // tm, zero_unused, 0)

    @pl.when(i >= 2)
    def _():
        wait_rows(slot)

    xn_ref[slot] = _rmsnorm_rows(h_ref[...], g_ref[...])

    def scatter(t, carry):
        src = xn_ref.at[slot, pl.ds(t, 1), :]
        pltpu.make_async_copy(src, xs_ref.at[pl.ds(d0_ref[t], 1), :], sem.at[slot]).start(priority=0)
        pltpu.make_async_copy(src, xs_ref.at[pl.ds(d1_ref[t], 1), :], sem.at[slot]).start(priority=1)
        return carry

    lax.fori_loop(0, tr, scatter, 0, unroll=8)

    @pl.when(i == n_steps - 1)
    def _():
        wait_rows(slot)
        if n_steps > 1:
            wait_rows(1 - slot)


def _dispatch(ends_p, n_pad, dest0, dest1, h1, g, n_rows, tr, tm):
    T, D = h1.shape
    n_steps = T // tr
    gs = pltpu.PrefetchScalarGridSpec(
        num_scalar_prefetch=2,
        grid=(n_steps,),
        in_specs=[pl.BlockSpec((tr,), lambda i, ep, npd: (i,), memory_space=pltpu.SMEM),
                  pl.BlockSpec((tr,), lambda i, ep, npd: (i,), memory_space=pltpu.SMEM),
                  pl.BlockSpec((tr, D), lambda i, ep, npd: (i, 0)),
                  pl.BlockSpec((1, D), lambda i, ep, npd: (0, 0))],
        out_specs=pl.BlockSpec(memory_space=pl.ANY),
        scratch_shapes=[pltpu.VMEM((2, tr, D), F32), pltpu.VMEM((tm, D), F32),
                        pltpu.SemaphoreType.DMA((2,)), pltpu.SemaphoreType.DMA(())],
    )
    return pl.pallas_call(
        functools.partial(_dispatch_body, tr=tr, tm=tm, n_steps=n_steps),
        grid_spec=gs,
        out_shape=jax.ShapeDtypeStruct((n_rows, D), F32),
        compiler_params=_cparams(("arbitrary",)),
        name="dispatch",
    )(ends_p, n_pad, dest0, dest1, h1, g.reshape(1, D))


def _moe_body(te_ref, nu_ref, ne_ref, cb_ref, ce_ref, sl_ref,
              x_ref, wg_hbm, wu_hbm, wd_hbm, y_ref,
              wg_s, wu_s, wd_s, sg, su, sd, sem):
    i = pl.program_id(0)
    slot = sl_ref[i]
    fu = wg_s.shape[-1] // W_UNITS

    def unit_copies(e, c, par):
        col = pl.multiple_of(c * fu, fu)
        return (pltpu.make_async_copy(wg_hbm.at[e, :, pl.ds(col, fu)], sg.at[par], sem.at[par]),
                pltpu.make_async_copy(wu_hbm.at[e, :, pl.ds(col, fu)], su.at[par], sem.at[par]),
                pltpu.make_async_copy(wd_hbm.at[e, pl.ds(col, fu), :], sd.at[par], sem.at[par]))

    def start_unit(e, c, par):
        for cp in unit_copies(e, c, par):
            cp.start()

    def wait_unit(e, c, par):
        for cp in unit_copies(e, c, par):
            cp.wait()

    def convert_unit(c, par, dst_slot):
        col = pl.multiple_of(c * fu, fu)
        wg_s[dst_slot, :, pl.ds(col, fu)] = sg[par].astype(BF16)
        wu_s[dst_slot, :, pl.ds(col, fu)] = su[par].astype(BF16)
        wd_s[dst_slot, pl.ds(col, fu), :] = sd[par].astype(BF16)

    def load_units(e, c_lo, c_hi, dst_slot):
        def unit(c, carry):
            par = c % 2
            wait_unit(e, c, par)

            @pl.when(c + 1 < W_UNITS)
            def _():
                start_unit(e, c + 1, 1 - par)

            convert_unit(c, par, dst_slot)
            return carry

        lax.fori_loop(c_lo, c_hi, unit, 0)

    @pl.when(i == 0)
    def _():
        start_unit(te_ref[0], 0, 0)
        load_units(te_ref[0], 0, W_UNITS, slot)

    @pl.when(jnp.logical_and(cb_ref[i] == 0, ce_ref[i] > 0))
    def _():
        start_unit(ne_ref[i], 0, 0)

    @pl.when(i < nu_ref[0])
    def _():
        x = x_ref[...].astype(BF16)
        a = jnp.dot(x, wg_s[slot], preferred_element_type=F32)
        b = jnp.dot(x, wu_s[slot], preferred_element_type=F32)
        hid = (a * jax.nn.sigmoid(a) * b).astype(BF16)
        y_ref[...] = jnp.dot(hid, wd_s[slot], preferred_element_type=F32)

    @pl.when(i >= nu_ref[0])
    def _():
        y_ref[...] = jnp.zeros(y_ref.shape, y_ref.dtype)

    load_units(ne_ref[i], cb_ref[i], ce_ref[i], 1 - slot)


def _moe(plan, x_sorted, w_gate, w_up, w_down, tm):
    P, D = x_sorted.shape
    F = w_gate.shape[-1]
    fu = F // W_UNITS
    n_pref = len(plan)
    x_map = lambda i, te, nu, *_: (jnp.minimum(i, nu[0] - 1), 0)
    hbm = pl.BlockSpec(memory_space=pl.ANY)
    gs = pltpu.PrefetchScalarGridSpec(
        num_scalar_prefetch=n_pref,
        grid=(P // tm,),
        in_specs=[pl.BlockSpec((tm, D), x_map), hbm, hbm, hbm],
        out_specs=pl.BlockSpec((tm, D), lambda i, *_: (i, 0)),
        scratch_shapes=[pltpu.VMEM((2, D, F), BF16), pltpu.VMEM((2, D, F), BF16), pltpu.VMEM((2, F, D), BF16),
                        pltpu.VMEM((2, D, fu), F32), pltpu.VMEM((2, D, fu), F32), pltpu.VMEM((2, fu, D), F32),
                        pltpu.SemaphoreType.DMA((2,))],
    )
    return pl.pallas_call(
        _moe_body,
        grid_spec=gs,
        out_shape=jax.ShapeDtypeStruct((P, D), F32),
        compiler_params=_cparams(("arbitrary",)),
        name="moe",
    )(*plan, x_sorted, w_gate, w_up, w_down)


def _combine_body(d0_ref, d1_ref, h_ref, gw_ref, g_ref, ys_hbm, o_ref, yb_ref, sem, *, tr, th, final_norm):
    halves = tr // th

    def gather(hf):
        def one(t, carry):
            tok = hf * th + t
            pltpu.make_async_copy(ys_hbm.at[pl.ds(d0_ref[tok], 1), :], yb_ref.at[hf, 0, pl.ds(t, 1), :],
                                  sem.at[hf]).start(priority=0)
            pltpu.make_async_copy(ys_hbm.at[pl.ds(d1_ref[tok], 1), :], yb_ref.at[hf, 1, pl.ds(t, 1), :],
                                  sem.at[hf]).start(priority=1)
            return carry

        lax.fori_loop(0, th, one, 0, unroll=8)

    for hf in range(halves):
        gather(hf)
    for hf in range(halves):
        for k in range(TOP_K):
            pltpu.make_async_copy(ys_hbm.at[pl.ds(0, th), :], yb_ref.at[hf, k], sem.at[hf]).wait()
        rows = pl.ds(hf * th, th)
        gw = gw_ref[rows, :]
        h = h_ref[rows, :] + gw[:, 0:1] * yb_ref[hf, 0] + gw[:, 1:2] * yb_ref[hf, 1]
        if final_norm:
            h = _rmsnorm_rows(h, g_ref[...])
        o_ref[rows, :] = h


def _combine(dest0, dest1, h1, gwt, g, y_sorted, final_norm, tr, th):
    T, D = h1.shape
    smem = pl.BlockSpec((tr,), lambda i: (i,), memory_space=pltpu.SMEM)
    row = pl.BlockSpec((tr, D), lambda i: (i, 0))
    return pl.pallas_call(
        functools.partial(_combine_body, tr=tr, th=th, final_norm=final_norm),
        grid=(T // tr,),
        in_specs=[smem, smem, row, pl.BlockSpec((tr, LANES), lambda i: (i, 0)),
                  pl.BlockSpec((1, D), lambda i: (0, 0)), pl.BlockSpec(memory_space=pl.ANY)],
        out_specs=row,
        out_shape=jax.ShapeDtypeStruct((T, D), F32),
        scratch_shapes=[pltpu.VMEM((tr // th, TOP_K, th, D), F32), pltpu.SemaphoreType.DMA((tr // th,))],
        compiler_params=_cparams(("arbitrary",)),
        name="combine",
    )(dest0, dest1, h1, gwt, g.reshape(1, D), y_sorted)


def _router_weights(w_group, b_group, w_expert, b_expert):
    def pack(g, e):
        gap = jnp.zeros(g.shape[:-1] + (EXPERT_COL0 - N_GROUPS,), F32)
        tail = jnp.zeros(g.shape[:-1] + (ROUTER_COLS - EXPERT_COL0 - N_EXPERTS,), F32)
        return jnp.concatenate([g.astype(F32), gap, e.astype(F32), tail], axis=-1)

    w = pack(w_group, w_expert)
    b = pack(b_group[None, :], b_expert[None, :])
    w_hi = w.astype(BF16)
    w_lo = (w - w_hi.astype(F32)).astype(BF16)
    return jnp.concatenate([w_hi, w_lo], axis=1), b


def _tile_plan(counts, tm, n_tiles):
    i32 = jnp.int32
    eidx = jnp.arange(N_EXPERTS, dtype=i32)
    tiles_e = (counts + tm - 1) // tm
    n_pad = tiles_e * tm
    ends_p = jnp.cumsum(n_pad).astype(i32)
    starts_p = ends_p - n_pad
    cum_tiles = jnp.cumsum(tiles_e).astype(i32)
    n_used = cum_tiles[-1]
    used = tiles_e > 0
    last_e = jnp.max(jnp.where(used, eidx, 0))
    ti = jnp.arange(n_tiles, dtype=i32)
    te = jnp.minimum(jnp.sum((ti[:, None] >= cum_tiles[None, :]).astype(i32), axis=1), last_e)
    is_te = te[:, None] == eidx[None, :]
    pick = lambda v: jnp.sum(jnp.where(is_te, v[None, :], 0), axis=1)
    t_in = ti - pick(cum_tiles - tiles_e)
    t_n = jnp.maximum(pick(tiles_e), 1)
    nxt = jnp.min(jnp.where(used[None, :] & (eidx[None, :] > eidx[:, None]), eidx[None, :], N_EXPERTS), axis=1)
    nxt_te = pick(nxt)
    has_next = (nxt_te < N_EXPERTS) & (ti < n_used)
    ne = jnp.where(has_next, nxt_te, 0).astype(i32)
    cb = jnp.where(has_next, (W_UNITS * t_in) // t_n, 0).astype(i32)
    ce = jnp.where(has_next, (W_UNITS * (t_in + 1)) // t_n, 0).astype(i32)
    slot = (pick(jnp.cumsum(used.astype(i32)) - 1) % 2).astype(i32)
    moe_plan = (te.astype(i32), n_used.reshape(1).astype(i32), ne, cb, ce, slot)
    return starts_p, ends_p, n_pad.astype(i32), moe_plan


def kernel(x, norm_mix_g, w_in, lambda_q1, lambda_k1, lambda_q2, lambda_k2, subln_g, conv_w, conv_b,
           conv_ln_g, conv_ln_b, w_out, norm_ffn_g, w_group_router, b_group_router, w_expert_router,
           b_expert_router, w_gate, w_up, w_down, norm_final_g):
    B, S, D = x.shape
    T = B * S
    depth = w_in.shape[0]
    tm_proj = min(512, T)
    tm_out = min(512, T)
    tq = min(512, S)
    ts = min(512, S)
    tm_moe = 256
    tr = min(1024, T)
    slopes = 2.0 ** (-8.0 * (jnp.arange(N_HEADS, dtype=F32) + 1.0) / N_HEADS)

    h = x.reshape(T, D)
    for l in range(depth):
        lam0 = 0.8 - 0.6 * math.exp(-0.3 * l)
        lamv = jnp.stack([lambda_q1[l], lambda_k1[l], lambda_q2[l], lambda_k2[l]]).astype(F32)
        proj = _inproj(h, norm_mix_g[l], w_in[l].astype(BF16), tm_proj)
        proj3 = proj.reshape(B, S, proj.shape[1])
        attn = _attention(proj3, slopes, lamv, subln_g[l], lam0, tq)
        conv = _conv(proj3, conv_w[l], conv_b[l], conv_ln_g[l], conv_ln_b[l], ts, 64)
        w_o = w_out[l].astype(BF16)
        wr, br = _router_weights(w_group_router[l], b_group_router[l], w_expert_router[l], b_expert_router[l])
        h1, logits = _outproj(h, attn.reshape(T, ATTN_WIDTH), conv.reshape(T, -1),
                              w_o[:ATTN_WIDTH], w_o[ATTN_WIDTH:], norm_ffn_g[l], wr, br, tm_out)
        ids8, gwt, cnt = _route(logits, tr)
        n_rows = T * TOP_K + N_EXPERTS * tm_moe
        starts_p, ends_p, n_pad, moe_plan = _tile_plan(cnt[:, 0].astype(jnp.int32), tm_moe, n_rows // tm_moe)
        starts_rep = jnp.broadcast_to(starts_p.astype(F32)[:, None], (N_EXPERTS, LANES))
        dest8 = _plan(ids8, starts_rep, tr)
        dest0, dest1 = dest8[0], dest8[1]
        x_sorted = _dispatch(ends_p, n_pad, dest0, dest1, h1, norm_ffn_g[l], n_rows, tr, tm_moe)
        y_sorted = _moe(moe_plan, x_sorted, w_gate[l], w_up[l], w_down[l], tm_moe)
        h = _combine(dest0, dest1, h1, gwt, norm_final_g, y_sorted, l == depth - 1, tr, tr // 2)
    return h.reshape(B, S, D)
```

```python
import functools
import math

import jax
import jax.numpy as jnp
import numpy as np
from jax import lax
from jax.experimental import pallas as pl
from jax.experimental.pallas import tpu as pltpu

N_HEADS = 8
QK_DIM = 64
HEAD_DIM = 2 * QK_DIM
ATTN_WIDTH = N_HEADS * HEAD_DIM
CONV_TAPS = 31
CONV_HALO = 32
N_GROUPS = 4
EXPERTS_PER_GROUP = 8
N_EXPERTS = N_GROUPS * EXPERTS_PER_GROUP
TOP_K = 2
EPS = 1e-6
NEG_INF = -1e30
LOG2E = 1.4426950408889634
LANES = 128
SUBLANES = 8
ROUTER_COLS = LANES
EXPERT_COL0 = SUBLANES
W_UNITS = 8
V7X_VMEM_LIMIT = 56 * 1024 * 1024

F32 = jnp.float32
BF16 = jnp.bfloat16


def _cparams(sem):
    return pltpu.CompilerParams(dimension_semantics=sem, vmem_limit_bytes=V7X_VMEM_LIMIT)


def _rmsnorm_rows(h, g):
    ms = jnp.mean(h * h, axis=-1, keepdims=True)
    return h * lax.rsqrt(ms + EPS) * g


def _inproj_body(x_ref, g_ref, w_ref, o_ref):
    xn = _rmsnorm_rows(x_ref[...], g_ref[...]).astype(BF16)
    o_ref[...] = jnp.dot(xn, w_ref[...], preferred_element_type=F32).astype(o_ref.dtype)


def _inproj(x2, g, w, tm):
    T, D = x2.shape
    N = w.shape[1]
    return pl.pallas_call(
        _inproj_body,
        grid=(T // tm,),
        in_specs=[pl.BlockSpec((tm, D), lambda i: (i, 0)),
                  pl.BlockSpec((1, D), lambda i: (0, 0)),
                  pl.BlockSpec((D, N), lambda i: (0, 0), pipeline_mode=pl.Buffered(1))],
        out_specs=pl.BlockSpec((tm, N), lambda i: (i, 0)),
        out_shape=jax.ShapeDtypeStruct((T, N), BF16),
        compiler_params=_cparams(("parallel",)),
        name="inproj",
    )(x2, g.reshape(1, D), w)


def _attn_body(qi_ref, j_ref, slopes_ref, lamv_ref, q_ref, k_ref, v_ref, g_ref, o_ref,
               qc_ref, kb_ref, mask_ref, m_ref, acc_ref, sa_ref, sb_ref, *, tq, nq, lam0):
    h = pl.program_id(0)
    tk = tq
    nstep = nq * (nq + 1) // 2
    slope2 = slopes_ref[h] * LOG2E
    lv = lamv_ref[...]
    lam = (jnp.exp(jnp.sum(lv[0:1] * lv[1:2], axis=1, keepdims=True))
           - jnp.exp(jnp.sum(lv[2:3] * lv[3:4], axis=1, keepdims=True)) + lam0)

    lane = lax.broadcasted_iota(jnp.int32, (tq, HEAD_DIM), 1)
    zero = jnp.zeros((tq, HEAD_DIM), BF16)
    one_cols = jnp.where(lane < 3, 1.0, 0.0).astype(BF16)

    def prepare_q(i, carry):
        rows = pl.ds(pl.multiple_of(i * tq, tq), tq)
        q = (q_ref[rows, :].astype(F32) * (QK_DIM ** -0.5 * LOG2E)).astype(BF16)
        qc_ref[i, :tq, :HEAD_DIM] = jnp.where(lane < QK_DIM, q, zero)
        qc_ref[i, tq:, :HEAD_DIM] = jnp.where(lane >= QK_DIM, q, zero)
        qc_ref[i, :tq, HEAD_DIM:] = one_cols
        qc_ref[i, tq:, HEAD_DIM:] = one_cols
        return carry

    def prepare_bias(i, carry):
        rows = pl.ds(pl.multiple_of(i * tq, tq), tq)
        bias = slope2 * (i * tq + lax.broadcasted_iota(jnp.int32, (tq, HEAD_DIM), 0)).astype(F32)
        hi = bias.astype(BF16).astype(F32)
        mid = (bias - hi).astype(BF16).astype(F32)
        lo = bias - hi - mid
        cols = jnp.where(lane == 0, hi, jnp.where(lane == 1, mid, jnp.where(lane == 2, lo, 0.0)))
        kb_ref[rows, :] = cols.astype(BF16)
        return carry

    lax.fori_loop(0, nq, prepare_q, 0)

    @pl.when(pl.program_id(1) == 0)
    def _():
        lax.fori_loop(0, nq, prepare_bias, 0)
        qpos = lax.broadcasted_iota(jnp.int32, (tq, tk), 0)
        kpos = lax.broadcasted_iota(jnp.int32, (tq, tk), 1)
        mask_ref[...] = jnp.where(qpos >= kpos, 0.0, NEG_INF)

    m_ref[...] = jnp.full(m_ref.shape, NEG_INF, F32)
    acc_ref[...] = jnp.zeros(acc_ref.shape, F32)
    ones = jnp.ones((tk, HEAD_DIM), BF16)

    def scores(s_ref, t):
        rows = pl.ds(pl.multiple_of(j_ref[t] * tk, tk), tk)
        kext = jnp.concatenate([k_ref[rows, :], kb_ref[rows, :]], axis=1)
        s_ref[...] = lax.dot_general(qc_ref[qi_ref[t]], kext, (((1,), (1,)), ((), ())),
                                     preferred_element_type=F32)

    def accumulate(s_ref, t, diagonal):
        rows = pl.ds(pl.multiple_of(j_ref[t] * tk, tk), tk)
        vext = jnp.concatenate([v_ref[rows, :], ones], axis=1)
        s = s_ref[...]
        if diagonal:
            s = s + jnp.tile(mask_ref[...], (2, 1))
        m_old = m_ref[...]
        m_new = jnp.maximum(m_old, jnp.max(s, axis=1, keepdims=True))
        alpha = jnp.tile(jnp.exp2(m_old - m_new), (1, 2))
        p = jnp.exp2(s - jnp.tile(m_new, (1, tk // LANES))).astype(BF16)
        a = alpha * acc_ref[...] + jnp.dot(p, vext, preferred_element_type=F32)
        if diagonal:
            o = (a[:tq, :HEAD_DIM] / a[:tq, HEAD_DIM:] - lam * (a[tq:, :HEAD_DIM] / a[tq:, HEAD_DIM:]))
            ms = jnp.mean(o * o, axis=1, keepdims=True)
            o_ref[pl.ds(pl.multiple_of(qi_ref[t] * tq, tq), tq), :] = (
                o * lax.rsqrt(ms + EPS) * g_ref[...] * (1.0 - lam0)).astype(o_ref.dtype)
            m_ref[...] = jnp.full(m_ref.shape, NEG_INF, F32)
        else:
            acc_ref[...] = a
            m_ref[...] = m_new

    def two_steps(t, diag0, diag1, trailing_scores=True):
        scores(sb_ref, t + 1)
        accumulate(sa_ref, t, diag0)
        if trailing_scores:
            scores(sa_ref, t + 2)
        accumulate(sb_ref, t + 1, diag1)

    def pair(i, carry):
        t = 2 * i
        d0 = j_ref[t] == qi_ref[t]
        d1 = j_ref[t + 1] == qi_ref[t + 1]

        @pl.when(d0)
        def _():
            two_steps(t, True, False)

        @pl.when(jnp.logical_and(jnp.logical_not(d0), d1))
        def _():
            two_steps(t, False, True)

        @pl.when(jnp.logical_not(jnp.logical_or(d0, d1)))
        def _():
            two_steps(t, False, False)

        return carry

    scores(sa_ref, 0)
    if nstep % 2 == 0:
        lax.fori_loop(0, nstep // 2 - 1, pair, 0)
        two_steps(nstep - 2, False, True, trailing_scores=False)
    else:
        lax.fori_loop(0, nstep // 2, pair, 0)
        accumulate(sa_ref, nstep - 1, True)


def _attention(proj3, slopes, lamv, subln_g, lam0, tq):
    B, S, _ = proj3.shape
    nq = S // tq
    steps = [(qi, j) for qi in range(nq) for j in range(qi + 1)]
    qi_tab = jnp.asarray([s[0] for s in steps], jnp.int32)
    j_tab = jnp.asarray([s[1] for s in steps], jnp.int32)
    body = functools.partial(_attn_body, tq=tq, nq=nq, lam0=lam0)
    seq = lambda col0: pl.BlockSpec((None, S, HEAD_DIM), lambda h, b, *_: (b, 0, col0 + h))
    gs = pltpu.PrefetchScalarGridSpec(
        num_scalar_prefetch=2,
        grid=(N_HEADS, B),
        in_specs=[pl.BlockSpec(memory_space=pltpu.SMEM),
                  pl.BlockSpec((4, QK_DIM), lambda h, b, *_: (0, 0)),
                  seq(0), seq(N_HEADS), seq(2 * N_HEADS),
                  pl.BlockSpec((1, HEAD_DIM), lambda h, b, *_: (0, 0))],
        out_specs=seq(0),
        scratch_shapes=[pltpu.VMEM((nq, 2 * tq, 2 * HEAD_DIM), BF16), pltpu.VMEM((S, HEAD_DIM), BF16),
                        pltpu.VMEM((tq, tq), F32), pltpu.VMEM((2 * tq, LANES), F32), pltpu.VMEM((2 * tq, 2 * HEAD_DIM), F32),
                        pltpu.VMEM((2 * tq, tq), F32), pltpu.VMEM((2 * tq, tq), F32)],
    )
    return pl.pallas_call(
        body,
        grid_spec=gs,
        out_shape=jax.ShapeDtypeStruct((B, S, ATTN_WIDTH), BF16),
        compiler_params=_cparams(("arbitrary", "arbitrary")),
        name="diffattn",
    )(qi_tab, j_tab, slopes, lamv, proj3, proj3, proj3, subln_g.reshape(1, HEAD_DIM))


def _conv_body(a_ref, g_ref, ah_ref, gh_ref, w_ref, cb_ref, lng_ref, lnb_ref, o_ref, u_ref, c_ref,
               *, ts, rb):
    i = pl.program_id(1)
    nchunk = u_ref.shape[0]
    u = a_ref[...].astype(F32) * jax.nn.sigmoid(g_ref[...].astype(F32))
    uh = ah_ref[...].astype(F32) * jax.nn.sigmoid(gh_ref[...].astype(F32))
    uh = jnp.where(i > 0, uh, 0.0)
    for c in range(nchunk):
        u_ref[c, pl.ds(0, CONV_HALO), :] = uh[:, c * LANES:(c + 1) * LANES]
        u_ref[c, pl.ds(CONV_HALO, ts), :] = u[:, c * LANES:(c + 1) * LANES]

    first = CONV_HALO - (CONV_TAPS - 1)

    def lane_chunk(c, carry):
        for r0 in range(0, ts, rb):
            acc = jnp.zeros((rb, LANES), F32)
            for k in range(CONV_TAPS):
                acc = acc + jnp.tile(w_ref[c, k], (rb // SUBLANES, 1)) * u_ref[c, pl.ds(r0 + first + k, rb), :]
            c_ref[c, pl.ds(r0, rb), :] = acc
        return carry

    lax.fori_loop(0, nchunk, lane_chunk, 0)

    C = nchunk * LANES
    cs = [c_ref[c] + cb_ref[:, c * LANES:(c + 1) * LANES] for c in range(nchunk)]
    mean = jnp.sum(functools.reduce(jnp.add, cs), axis=1, keepdims=True) * (1.0 / C)
    xcs = [x - mean for x in cs]
    var = jnp.sum(functools.reduce(jnp.add, [x * x for x in xcs]), axis=1, keepdims=True) * (1.0 / C)
    rstd = lax.rsqrt(var + EPS)
    for c in range(nchunk):
        sl = slice(c * LANES, (c + 1) * LANES)
        y = xcs[c] * rstd * lng_ref[:, sl] + lnb_ref[:, sl]
        o_ref[:, sl] = (y * jax.nn.sigmoid(y)).astype(o_ref.dtype)


def _conv(proj3, conv_w, conv_b, ln_g, ln_b, ts, rb):
    B, S, _ = proj3.shape
    C = conv_w.shape[-1]
    nchunk = C // LANES
    a_blk = (3 * ATTN_WIDTH) // C
    hpt = ts // CONV_HALO
    body = functools.partial(_conv_body, ts=ts, rb=rb)
    w_b = jnp.broadcast_to(conv_w.reshape(CONV_TAPS, nchunk, 1, LANES).transpose(1, 0, 2, 3),
                           (nchunk, CONV_TAPS, SUBLANES, LANES))
    halo = lambda blk: (lambda b, i: (b, jnp.maximum(i * hpt - 1, 0), blk))
    vec = pl.BlockSpec((1, C), lambda b, i: (0, 0))
    return pl.pallas_call(
        body,
        grid=(B, S // ts),
        in_specs=[pl.BlockSpec((None, ts, C), lambda b, i: (b, i, a_blk)),
                  pl.BlockSpec((None, ts, C), lambda b, i: (b, i, a_blk + 1)),
                  pl.BlockSpec((None, CONV_HALO, C), halo(a_blk)),
                  pl.BlockSpec((None, CONV_HALO, C), halo(a_blk + 1)),
                  pl.BlockSpec((nchunk, CONV_TAPS, SUBLANES, LANES), lambda b, i: (0, 0, 0, 0)),
                  vec, vec, vec],
        out_specs=pl.BlockSpec((None, ts, C), lambda b, i: (b, i, 0)),
        out_shape=jax.ShapeDtypeStruct((B, S, C), BF16),
        scratch_shapes=[pltpu.VMEM((nchunk, ts + CONV_HALO, LANES), F32), pltpu.VMEM((nchunk, ts, LANES), F32)],
        compiler_params=_cparams(("parallel", "arbitrary")),
        name="convmod",
    )(proj3, proj3, proj3, proj3, w_b, conv_b.reshape(1, C), ln_g.reshape(1, C), ln_b.reshape(1, C))


def _outproj_body(x_ref, a_ref, c_ref, wa_ref, wc_ref, g_ref, wr_ref, br_ref, h_ref, lg_ref, *, parts):
    rp = x_ref.shape[0] // parts
    for s in range(parts):
        rows = slice(s * rp, (s + 1) * rp)
        acc = (jnp.dot(a_ref[rows, :], wa_ref[...], preferred_element_type=F32)
               + jnp.dot(c_ref[rows, :], wc_ref[...], preferred_element_type=F32))
        h = x_ref[rows, :] + acc
        h_ref[rows, :] = h
        xn = _rmsnorm_rows(h, g_ref[...])
        hi = xn.astype(BF16)
        lo = (xn - hi.astype(F32)).astype(BF16)
        r = (jnp.dot(hi, wr_ref[...], preferred_element_type=F32)
             + jnp.dot(lo, wr_ref[...], preferred_element_type=F32))
        lg_ref[rows, :] = r[:, :ROUTER_COLS] + r[:, ROUTER_COLS:] + br_ref[...]


def _outproj(x2, attn2, conv2, w_attn, w_conv, g, wr, br, tm):
    T, D = x2.shape
    Ka, Kc = attn2.shape[1], conv2.shape[1]
    row = lambda n: pl.BlockSpec((tm, n), lambda i: (i, 0))
    full = lambda a, b: pl.BlockSpec((a, b), lambda i: (0, 0))
    return pl.pallas_call(
        functools.partial(_outproj_body, parts=4 if tm % 32 == 0 else 1),
        grid=(T // tm,),
        in_specs=[row(D), row(Ka), row(Kc), full(Ka, D), full(Kc, D), full(1, D),
                  full(D, 2 * ROUTER_COLS), full(1, ROUTER_COLS)],
        out_specs=[row(D), row(ROUTER_COLS)],
        out_shape=[jax.ShapeDtypeStruct((T, D), F32), jax.ShapeDtypeStruct((T, ROUTER_COLS), F32)],
        compiler_params=_cparams(("parallel",)),
        name="outproj",
    )(x2, attn2, conv2, w_attn, w_conv, g.reshape(1, D), wr, br)


def _route_body(lg_ref, id_ref, gw_ref, cnt_ref):
    lt = lg_ref[...].T
    tr = lt.shape[1]
    row = lax.broadcasted_iota(jnp.int32, (SUBLANES, tr), 0)
    gl = jnp.where(row < N_GROUPS, lt[0:SUBLANES], -jnp.inf)
    gmax = jnp.max(gl, axis=0, keepdims=True)
    g_sel = jnp.min(jnp.where(gl == gmax, row, SUBLANES), axis=0, keepdims=True)
    g_w = 1.0 / jnp.sum(jnp.exp(gl - gmax), axis=0, keepdims=True)
    e_in = lt[EXPERT_COL0:EXPERT_COL0 + EXPERTS_PER_GROUP]
    for g in range(1, N_GROUPS):
        lo = EXPERT_COL0 + g * EXPERTS_PER_GROUP
        e_in = jnp.where(g_sel == g, lt[lo:lo + EXPERTS_PER_GROUP], e_in)
    v1 = jnp.max(e_in, axis=0, keepdims=True)
    i1 = jnp.min(jnp.where(e_in == v1, row, SUBLANES), axis=0, keepdims=True)
    e_rest = jnp.where(row == i1, -jnp.inf, e_in)
    v2 = jnp.max(e_rest, axis=0, keepdims=True)
    i2 = jnp.min(jnp.where(e_rest == v2, row, SUBLANES), axis=0, keepdims=True)
    d = jnp.exp(v2 - v1)
    w1 = g_w / (1.0 + d)
    w2 = w1 * d
    e1 = g_sel * EXPERTS_PER_GROUP + i1
    e2 = g_sel * EXPERTS_PER_GROUP + i2
    id_ref[...] = jnp.where(row == 0, e1, jnp.where(row == 1, e2, 0))
    rowl = lax.broadcasted_iota(jnp.int32, (LANES, tr), 0)
    gw_ref[...] = jnp.where(rowl == 0, w1, jnp.where(rowl == 1, w2, 0.0)).T
    rowe = lax.broadcasted_iota(jnp.int32, (N_EXPERTS, tr), 0)
    hits = jnp.where(rowe == e1, 1.0, 0.0) + jnp.where(rowe == e2, 1.0, 0.0)

    @pl.when(pl.program_id(0) == 0)
    def _():
        cnt_ref[...] = jnp.zeros(cnt_ref.shape, F32)

    cnt_ref[...] += jnp.broadcast_to(jnp.sum(hits, axis=1, keepdims=True), cnt_ref.shape)


def _route(logits, tr):
    T = logits.shape[0]
    return pl.pallas_call(
        _route_body,
        grid=(T // tr,),
        in_specs=[pl.BlockSpec((tr, ROUTER_COLS), lambda i: (i, 0))],
        out_specs=[pl.BlockSpec((SUBLANES, tr), lambda i: (0, i)),
                   pl.BlockSpec((tr, LANES), lambda i: (i, 0)),
                   pl.BlockSpec((N_EXPERTS, LANES), lambda i: (0, 0))],
        out_shape=[jax.ShapeDtypeStruct((SUBLANES, T), jnp.int32),
                   jax.ShapeDtypeStruct((T, LANES), F32),
                   jax.ShapeDtypeStruct((N_EXPERTS, LANES), F32)],
        compiler_params=_cparams(("arbitrary",)),
        name="route",
    )(logits)


def _plan_body(id_ref, st_ref, u_ref, d_ref, base_ref):
    @pl.when(pl.program_id(0) == 0)
    def _():
        base_ref[...] = st_ref[...]

    ids = id_ref[...]
    tr = ids.shape[1]
    rowe = lax.broadcasted_iota(jnp.int32, (N_EXPERTS, tr), 0)
    base = base_ref[:, 0:1]
    dests = []
    for k in range(TOP_K):
        hit = rowe == ids[k:k + 1]
        before = jnp.dot(jnp.where(hit, 1.0, 0.0).astype(BF16), u_ref[...], preferred_element_type=F32)
        dests.append(jnp.sum(jnp.where(hit, base + before, 0.0), axis=0, keepdims=True))
        base = base + jnp.sum(jnp.where(hit, 1.0, 0.0), axis=1, keepdims=True)
    base_ref[...] = jnp.broadcast_to(base, base_ref.shape)
    row = lax.broadcasted_iota(jnp.int32, (SUBLANES, tr), 0)
    d_ref[...] = jnp.where(row == 0, dests[0], jnp.where(row == 1, dests[1], 0.0)).astype(jnp.int32)


def _plan(ids8, starts_rep, tr):
    T = ids8.shape[1]
    upper = jnp.asarray(np.triu(np.ones((tr, tr), np.float32), 1), BF16)
    return pl.pallas_call(
        _plan_body,
        grid=(T // tr,),
        in_specs=[pl.BlockSpec((SUBLANES, tr), lambda i: (0, i)),
                  pl.BlockSpec((N_EXPERTS, LANES), lambda i: (0, 0)),
                  pl.BlockSpec((tr, tr), lambda i: (0, 0))],
        out_specs=pl.BlockSpec((SUBLANES, tr), lambda i: (0, i)),
        out_shape=jax.ShapeDtypeStruct((SUBLANES, T), jnp.int32),
        scratch_shapes=[pltpu.VMEM((N_EXPERTS, LANES), F32)],
        compiler_params=_cparams(("arbitrary",)),
        name="plan",
    )(ids8, starts_rep, upper)


def _dispatch_body(endp_ref, npad_ref, d0_ref, d1_ref, h_ref, g_ref, xs_ref, xn_ref, zb_ref, sem, zsem,
                   *, tr, tm, n_steps):
    i = pl.program_id(0)
    slot = i % 2

    def wait_rows(s):
        for _ in range(TOP_K):
            pltpu.make_async_copy(xn_ref.at[s], xs_ref.at[pl.ds(0, tr), :], sem.at[s]).wait()

    @pl.when(i == 0)
    def _():
        zb_ref[...] = jnp.zeros(zb_ref.shape, F32)

        def zero_tail(e, carry):
            @pl.when(npad_ref[e] > 0)
            def _():
                tail = pl.multiple_of(endp_ref[e] - tm, tm)
                cp = pltpu.make_async_copy(zb_ref, xs_ref.at[pl.ds(tail, tm), :], zsem)
                cp.start()
                cp.wait()
            return carry

        lax.fori_loop(0, N_EXPERTS, zero_tail, 0)

        def zero_unused(j, carry):
            cp = pltpu.make_async_copy(zb_ref, xs_ref.at[pl.ds(pl.multiple_of(j * tm, tm), tm), :], zsem)
            cp.start()
            cp.wait()
            return carry

        lax.fori_loop(endp_ref[N_EXPERTS - 1] // tm, xs_ref.shape[0] // tm, zero_unused, 0)

    @pl.when(i >= 2)
    def _():
        wait_rows(slot)

    xn_ref[slot] = _rmsnorm_rows(h_ref[...], g_ref[...])

    def scatter(t, carry):
        src = xn_ref.at[slot, pl.ds(t, 1), :]
        pltpu.make_async_copy(src, xs_ref.at[pl.ds(d0_ref[t], 1), :], sem.at[slot]).start(priority=0)
        pltpu.make_async_copy(src, xs_ref.at[pl.ds(d1_ref[t], 1), :], sem.at[slot]).start(priority=1)
        return carry

    lax.fori_loop(0, tr, scatter, 0, unroll=8)

    @pl.when(i == n_steps - 1)
    def _():
        wait_rows(slot)
        if n_steps > 1:
            wait_rows(1 - slot)


def _dispatch(ends_p, n_pad, dest0, dest1, h1, g, n_rows, tr, tm):
    T, D = h1.shape
    n_steps = T // tr
    gs = pltpu.PrefetchScalarGridSpec(
        num_scalar_prefetch=2,
        grid=(n_steps,),
        in_specs=[pl.BlockSpec((tr,), lambda i, ep, npd: (i,), memory_space=pltpu.SMEM),
                  pl.BlockSpec((tr,), lambda i, ep, npd: (i,), memory_space=pltpu.SMEM),
                  pl.BlockSpec((tr, D), lambda i, ep, npd: (i, 0)),
                  pl.BlockSpec((1, D), lambda i, ep, npd: (0, 0))],
        out_specs=pl.BlockSpec(memory_space=pl.ANY),
        scratch_shapes=[pltpu.VMEM((2, tr, D), F32), pltpu.VMEM((tm, D), F32),
                        pltpu.SemaphoreType.DMA((2,)), pltpu.SemaphoreType.DMA(())],
    )
    return pl.pallas_call(
        functools.partial(_dispatch_body, tr=tr, tm=tm, n_steps=n_steps),
        grid_spec=gs,
        out_shape=jax.ShapeDtypeStruct((n_rows, D), F32),
        compiler_params=_cparams(("arbitrary",)),
        name="dispatch",
    )(ends_p, n_pad, dest0, dest1, h1, g.reshape(1, D))


def _moe_body(te_ref, nu_ref, ne_ref, cb_ref, ce_ref, sl_ref,
              x_ref, wg_hbm, wu_hbm, wd_hbm, y_ref,
              wg_s, wu_s, wd_s, sg, su, sd, sem):
    i = pl.program_id(0)
    slot = sl_ref[i]
    fu = wg_s.shape[-1] // W_UNITS

    def unit_copies(e, c, par):
        col = pl.multiple_of(c * fu, fu)
        return (pltpu.make_async_copy(wg_hbm.at[e, :, pl.ds(col, fu)], sg.at[par], sem.at[par]),
                pltpu.make_async_copy(wu_hbm.at[e, :, pl.ds(col, fu)], su.at[par], sem.at[par]),
                pltpu.make_async_copy(wd_hbm.at[e, pl.ds(col, fu), :], sd.at[par], sem.at[par]))

    def start_unit(e, c, par):
        for cp in unit_copies(e, c, par):
            cp.start()

    def wait_unit(e, c, par):
        for cp in unit_copies(e, c, par):
            cp.wait()

    def convert_unit(c, par, dst_slot):
        col = pl.multiple_of(c * fu, fu)
        wg_s[dst_slot, :, pl.ds(col, fu)] = sg[par].astype(BF16)
        wu_s[dst_slot, :, pl.ds(col, fu)] = su[par].astype(BF16)
        wd_s[dst_slot, pl.ds(col, fu), :] = sd[par].astype(BF16)

    def load_units(e, c_lo, c_hi, dst_slot):
        def unit(c, carry):
            par = c % 2
            wait_unit(e, c, par)

            @pl.when(c + 1 < W_UNITS)
            def _():
                start_unit(e, c + 1, 1 - par)

            convert_unit(c, par, dst_slot)
            return carry

        lax.fori_loop(c_lo, c_hi, unit, 0)

    @pl.when(i == 0)
    def _():
        start_unit(te_ref[0], 0, 0)
        load_units(te_ref[0], 0, W_UNITS, slot)

    @pl.when(jnp.logical_and(cb_ref[i] == 0, ce_ref[i] > 0))
    def _():
        start_unit(ne_ref[i], 0, 0)

    @pl.when(i < nu_ref[0])
    def _():
        x = x_ref[...].astype(BF16)
        a = jnp.dot(x, wg_s[slot], preferred_element_type=F32)
        b = jnp.dot(x, wu_s[slot], preferred_element_type=F32)
        hid = (a * jax.nn.sigmoid(a) * b).astype(BF16)
        y_ref[...] = jnp.dot(hid, wd_s[slot], preferred_element_type=F32)

    @pl.when(i >= nu_ref[0])
    def _():
        y_ref[...] = jnp.zeros(y_ref.shape, y_ref.dtype)

    load_units(ne_ref[i], cb_ref[i], ce_ref[i], 1 - slot)


def _moe(plan, x_sorted, w_gate, w_up, w_down, tm):
    P, D = x_sorted.shape
    F = w_gate.shape[-1]
    fu = F // W_UNITS
    n_pref = len(plan)
    x_map = lambda i, te, nu, *_: (jnp.minimum(i, nu[0] - 1), 0)
    hbm = pl.BlockSpec(memory_space=pl.ANY)
    gs = pltpu.PrefetchScalarGridSpec(
        num_scalar_prefetch=n_pref,
        grid=(P // tm,),
        in_specs=[pl.BlockSpec((tm, D), x_map), hbm, hbm, hbm],
        out_specs=pl.BlockSpec((tm, D), lambda i, *_: (i, 0)),
        scratch_shapes=[pltpu.VMEM((2, D, F), BF16), pltpu.VMEM((2, D, F), BF16), pltpu.VMEM((2, F, D), BF16),
                        pltpu.VMEM((2, D, fu), F32), pltpu.VMEM((2, D, fu), F32), pltpu.VMEM((2, fu, D), F32),
                        pltpu.SemaphoreType.DMA((2,))],
    )
    return pl.pallas_call(
        _moe_body,
        grid_spec=gs,
        out_shape=jax.ShapeDtypeStruct((P, D), F32),
        compiler_params=_cparams(("arbitrary",)),
        name="moe",
    )(*plan, x_sorted, w_gate, w_up, w_down)


def _combine_body(d0_ref, d1_ref, h_ref, gw_ref, g_ref, ys_hbm, o_ref, yb_ref, sem, *, tr, th, final_norm):
    halves = tr // th

    def gather(hf):
        def one(t, carry):
            tok = hf * th + t
            pltpu.make_async_copy(ys_hbm.at[pl.ds(d0_ref[tok], 1), :], yb_ref.at[hf, 0, pl.ds(t, 1), :],
                                  sem.at[hf]).start(priority=0)
            pltpu.make_async_copy(ys_hbm.at[pl.ds(d1_ref[tok], 1), :], yb_ref.at[hf, 1, pl.ds(t, 1), :],
                                  sem.at[hf]).start(priority=1)
            return carry

        lax.fori_loop(0, th, one, 0, unroll=8)

    for hf in range(halves):
        gather(hf)
    for hf in range(halves):
        for k in range(TOP_K):
            pltpu.make_async_copy(ys_hbm.at[pl.ds(0, th), :], yb_ref.at[hf, k], sem.at[hf]).wait()
        rows = pl.ds(hf * th, th)
        gw = gw_ref[rows, :]
        h = h_ref[rows, :] + gw[:, 0:1] * yb_ref[hf, 0] + gw[:, 1:2] * yb_ref[hf, 1]
        if final_norm:
            h = _rmsnorm_rows(h, g_ref[...])
        o_ref[rows, :] = h


def _combine(dest0, dest1, h1, gwt, g, y_sorted, final_norm, tr, th):
    T, D = h1.shape
    smem = pl.BlockSpec((tr,), lambda i: (i,), memory_space=pltpu.SMEM)
    row = pl.BlockSpec((tr, D), lambda i: (i, 0))
    return pl.pallas_call(
        functools.partial(_combine_body, tr=tr, th=th, final_norm=final_norm),
        grid=(T // tr,),
        in_specs=[smem, smem, row, pl.BlockSpec((tr, LANES), lambda i: (i, 0)),
                  pl.BlockSpec((1, D), lambda i: (0, 0)), pl.BlockSpec(memory_space=pl.ANY)],
        out_specs=row,
        out_shape=jax.ShapeDtypeStruct((T, D), F32),
        scratch_shapes=[pltpu.VMEM((tr // th, TOP_K, th, D), F32), pltpu.SemaphoreType.DMA((tr // th,))],
        compiler_params=_cparams(("arbitrary",)),
        name="combine",
    )(dest0, dest1, h1, gwt, g.reshape(1, D), y_sorted)


def _router_weights(w_group, b_group, w_expert, b_expert):
    def pack(g, e):
        gap = jnp.zeros(g.shape[:-1] + (EXPERT_COL0 - N_GROUPS,), F32)
        tail = jnp.zeros(g.shape[:-1] + (ROUTER_COLS - EXPERT_COL0 - N_EXPERTS,), F32)
        return jnp.concatenate([g.astype(F32), gap, e.astype(F32), tail], axis=-1)

    w = pack(w_group, w_expert)
    b = pack(b_group[None, :], b_expert[None, :])
    w_hi = w.astype(BF16)
    w_lo = (w - w_hi.astype(F32)).astype(BF16)
    return jnp.concatenate([w_hi, w_lo], axis=1), b


def _tile_plan(counts, tm, n_tiles):
    i32 = jnp.int32
    eidx = jnp.arange(N_EXPERTS, dtype=i32)
    tiles_e = (counts + tm - 1) // tm
    n_pad = tiles_e * tm
    ends_p = jnp.cumsum(n_pad).astype(i32)
    starts_p = ends_p - n_pad
    cum_tiles = jnp.cumsum(tiles_e).astype(i32)
    n_used = cum_tiles[-1]
    used = tiles_e > 0
    last_e = jnp.max(jnp.where(used, eidx, 0))
    ti = jnp.arange(n_tiles, dtype=i32)
    te = jnp.minimum(jnp.sum((ti[:, None] >= cum_tiles[None, :]).astype(i32), axis=1), last_e)
    is_te = te[:, None] == eidx[None, :]
    pick = lambda v: jnp.sum(jnp.where(is_te, v[None, :], 0), axis=1)
    t_in = ti - pick(cum_tiles - tiles_e)
    t_n = jnp.maximum(pick(tiles_e), 1)
    nxt = jnp.min(jnp.where(used[None, :] & (eidx[None, :] > eidx[:, None]), eidx[None, :], N_EXPERTS), axis=1)
    nxt_te = pick(nxt)
    has_next = (nxt_te < N_EXPERTS) & (ti < n_used)
    ne = jnp.where(has_next, nxt_te, 0).astype(i32)
    cb = jnp.where(has_next, (W_UNITS * t_in) // t_n, 0).astype(i32)
    ce = jnp.where(has_next, (W_UNITS * (t_in + 1)) // t_n, 0).astype(i32)
    slot = (pick(jnp.cumsum(used.astype(i32)) - 1) % 2).astype(i32)
    moe_plan = (te.astype(i32), n_used.reshape(1).astype(i32), ne, cb, ce, slot)
    return starts_p, ends_p, n_pad.astype(i32), moe_plan


def kernel(x, norm_mix_g, w_in, lambda_q1, lambda_k1, lambda_q2, lambda_k2, subln_g, conv_w, conv_b,
           conv_ln_g, conv_ln_b, w_out, norm_ffn_g, w_group_router, b_group_router, w_expert_router,
           b_expert_router, w_gate, w_up, w_down, norm_final_g):
    B, S, D = x.shape
    T = B * S
    depth = w_in.shape[0]
    tm_proj = min(512, T)
    tm_out = min(512, T)
    tq = min(512, S)
    ts = min(512, S)
    tm_moe = 256
    tr = min(1024, T)
    slopes = 2.0 ** (-8.0 * (jnp.arange(N_HEADS, dtype=F32) + 1.0) / N_HEADS)

    h = x.reshape(T, D)
    for l in range(depth):
        lam0 = 0.8 - 0.6 * math.exp(-0.3 * l)
        lamv = jnp.stack([lambda_q1[l], lambda_k1[l], lambda_q2[l], lambda_k2[l]]).astype(F32)
        proj = _inproj(h, norm_mix_g[l], w_in[l].astype(BF16), tm_proj)
        proj3 = proj.reshape(B, S, proj.shape[1])
        attn = _attention(proj3, slopes, lamv, subln_g[l], lam0, tq)
        conv = _conv(proj3, conv_w[l], conv_b[l], conv_ln_g[l], conv_ln_b[l], ts, 64)
        w_o = w_out[l].astype(BF16)
        wr, br = _router_weights(w_group_router[l], b_group_router[l], w_expert_router[l], b_expert_router[l])
        h1, logits = _outproj(h, attn.reshape(T, ATTN_WIDTH), conv.reshape(T, -1),
                              w_o[:ATTN_WIDTH], w_o[ATTN_WIDTH:], norm_ffn_g[l], wr, br, tm_out)
        ids8, gwt, cnt = _route(logits, tr)
        n_rows = T * TOP_K + N_EXPERTS * tm_moe
        starts_p, ends_p, n_pad, moe_plan = _tile_plan(cnt[:, 0].astype(jnp.int32), tm_moe, n_rows // tm_moe)
        starts_rep = jnp.broadcast_to(starts_p.astype(F32)[:, None], (N_EXPERTS, LANES))
        dest8 = _plan(ids8, starts_rep, tr)
        dest0, dest1 = dest8[0], dest8[1]
        x_sorted = _dispatch(ends_p, n_pad, dest0, dest1, h1, norm_ffn_g[l], n_rows, tr, tm_moe)
        y_sorted = _moe(moe_plan, x_sorted, w_gate[l], w_up[l], w_down[l], tm_moe)
        h = _combine(dest0, dest1, h1, gwt, norm_final_g, y_sorted, l == depth - 1, tr, tr // 2)
    return h.reshape(B, S, D)
```

```python
import functools
import math

import jax
import jax.numpy as jnp
import numpy as np
from jax import lax
from jax.experimental import pallas as pl
from jax.experimental.pallas import tpu as pltpu

N_HEADS = 8
QK_DIM = 64
HEAD_DIM = 2 * QK_DIM
ATTN_WIDTH = N_HEADS * HEAD_DIM
CONV_TAPS = 31
CONV_HALO = 32
N_GROUPS = 4
EXPERTS_PER_GROUP = 8
N_EXPERTS = N_GROUPS * EXPERTS_PER_GROUP
TOP_K = 2
EPS = 1e-6
NEG_INF = -1e30
LOG2E = 1.4426950408889634
LANES = 128
SUBLANES = 8
ROUTER_COLS = LANES
EXPERT_COL0 = SUBLANES
W_UNITS = 8
V7X_VMEM_LIMIT = 56 * 1024 * 1024

F32 = jnp.float32
BF16 = jnp.bfloat16


def _cparams(sem):
    return pltpu.CompilerParams(dimension_semantics=sem, vmem_limit_bytes=V7X_VMEM_LIMIT)


def _rmsnorm_rows(h, g):
    ms = jnp.mean(h * h, axis=-1, keepdims=True)
    return h * lax.rsqrt(ms + EPS) * g


def _inproj_body(x_ref, g_ref, w_ref, o_ref):
    xn = _rmsnorm_rows(x_ref[...], g_ref[...]).astype(BF16)
    o_ref[...] = jnp.dot(xn, w_ref[...], preferred_element_type=F32).astype(o_ref.dtype)


def _inproj(x2, g, w, tm):
    T, D = x2.shape
    N = w.shape[1]
    return pl.pallas_call(
        _inproj_body,
        grid=(T // tm,),
        in_specs=[pl.BlockSpec((tm, D), lambda i: (i, 0)),
                  pl.BlockSpec((1, D), lambda i: (0, 0)),
                  pl.BlockSpec((D, N), lambda i: (0, 0), pipeline_mode=pl.Buffered(1))],
        out_specs=pl.BlockSpec((tm, N), lambda i: (i, 0)),
        out_shape=jax.ShapeDtypeStruct((T, N), BF16),
        compiler_params=_cparams(("parallel",)),
        name="inproj",
    )(x2, g.reshape(1, D), w)


def _attn_body(qi_ref, j_ref, slopes_ref, lamv_ref, q_ref, k_ref, v_ref, g_ref, o_ref,
               qc_ref, kb_ref, mask_ref, m_ref, acc_ref, sa_ref, sb_ref, *, tq, nq, lam0):
    h = pl.program_id(0)
    tk = tq
    nstep = nq * (nq + 1) // 2
    slope2 = slopes_ref[h] * LOG2E
    lv = lamv_ref[...]
    lam = (jnp.exp(jnp.sum(lv[0:1] * lv[1:2], axis=1, keepdims=True))
           - jnp.exp(jnp.sum(lv[2:3] * lv[3:4], axis=1, keepdims=True)) + lam0)

    lane = lax.broadcasted_iota(jnp.int32, (tq, HEAD_DIM), 1)
    zero = jnp.zeros((tq, HEAD_DIM), BF16)
    one_cols = jnp.where(lane < 3, 1.0, 0.0).astype(BF16)

    def prepare_q(i, carry):
        rows = pl.ds(pl.multiple_of(i * tq, tq), tq)
        q = (q_ref[rows, :].astype(F32) * (QK_DIM ** -0.5 * LOG2E)).astype(BF16)
        qc_ref[i, :tq, :HEAD_DIM] = jnp.where(lane < QK_DIM, q, zero)
        qc_ref[i, tq:, :HEAD_DIM] = jnp.where(lane >= QK_DIM, q, zero)
        qc_ref[i, :tq, HEAD_DIM:] = one_cols
        qc_ref[i, tq:, HEAD_DIM:] = one_cols
        return carry

    def prepare_bias(i, carry):
        rows = pl.ds(pl.multiple_of(i * tq, tq), tq)
        bias = slope2 * (i * tq + lax.broadcasted_iota(jnp.int32, (tq, HEAD_DIM), 0)).astype(F32)
        hi = bias.astype(BF16).astype(F32)
        mid = (bias - hi).astype(BF16).astype(F32)
        lo = bias - hi - mid
        cols = jnp.where(lane == 0, hi, jnp.where(lane == 1, mid, jnp.where(lane == 2, lo, 0.0)))
        kb_ref[rows, :] = cols.astype(BF16)
        return carry

    lax.fori_loop(0, nq, prepare_q, 0)

    @pl.when(pl.program_id(1) == 0)
    def _():
        lax.fori_loop(0, nq, prepare_bias, 0)
        qpos = lax.broadcasted_iota(jnp.int32, (tq, tk), 0)
        kpos = lax.broadcasted_iota(jnp.int32, (tq, tk), 1)
        mask_ref[...] = jnp.where(qpos >= kpos, 0.0, NEG_INF)

    m_ref[...] = jnp.full(m_ref.shape, NEG_INF, F32)

    @pl.when(jnp.logical_and(pl.program_id(0) == 0, pl.program_id(1) == 0))
    def _():
        acc_ref[...] = jnp.zeros(acc_ref.shape, F32)

    ones = jnp.ones((tk, HEAD_DIM), BF16)

    def scores(s_ref, t):
        rows = pl.ds(pl.multiple_of(j_ref[t] * tk, tk), tk)
        kext = jnp.concatenate([k_ref[rows, :], kb_ref[rows, :]], axis=1)
        s_ref[...] = lax.dot_general(qc_ref[qi_ref[t]], kext, (((1,), (1,)), ((), ())),
                                     preferred_element_type=F32)

    def accumulate(s_ref, t, diagonal):
        rows = pl.ds(pl.multiple_of(j_ref[t] * tk, tk), tk)
        vext = jnp.concatenate([v_ref[rows, :], ones], axis=1)
        s = s_ref[...]
        if diagonal:
            s = s + jnp.tile(mask_ref[...], (2, 1))
        qi = qi_ref[t]
        m_old = m_ref[qi]
        m_new = jnp.maximum(m_old, jnp.max(s, axis=1, keepdims=True))
        alpha = jnp.tile(jnp.exp2(m_old - m_new), (1, 2))
        p = jnp.exp2(s - jnp.tile(m_new, (1, tk // LANES))).astype(BF16)
        a = alpha * acc_ref[qi] + jnp.dot(p, vext, preferred_element_type=F32)
        if diagonal:
            o = (a[:tq, :HEAD_DIM] / a[:tq, HEAD_DIM:] - lam * (a[tq:, :HEAD_DIM] / a[tq:, HEAD_DIM:]))
            ms = jnp.mean(o * o, axis=1, keepdims=True)
            o_ref[pl.ds(pl.multiple_of(qi * tq, tq), tq), :] = (
                o * lax.rsqrt(ms + EPS) * g_ref[...] * (1.0 - lam0)).astype(o_ref.dtype)
        else:
            acc_ref[qi] = a
            m_ref[qi] = m_new

    bufs = (sa_ref, sb_ref)
    n_full = nstep - nq

    def step(t, par, diagonal, has_next=True):
        if has_next:
            scores(bufs[1 - par], t + 1)
        accumulate(bufs[par], t, diagonal)

    def four_steps(i, carry):
        for u in range(4):
            step(4 * i + u, u % 2, False)
        return carry

    scores(sa_ref, 0)
    lax.fori_loop(0, n_full // 4, four_steps, 0)
    for t in range(4 * (n_full // 4), n_full):
        step(t, t % 2, False)
    for t in range(n_full, nstep):
        step(t, t % 2, True, has_next=t + 1 < nstep)


def _attention(proj3, slopes, lamv, subln_g, lam0, tq):
    B, S, _ = proj3.shape
    nq = S // tq
    steps = [(qi, j) for qi in range(nq) for j in range(qi)] + [(qi, qi) for qi in range(nq)]
    qi_tab = jnp.asarray([s[0] for s in steps], jnp.int32)
    j_tab = jnp.asarray([s[1] for s in steps], jnp.int32)
    body = functools.partial(_attn_body, tq=tq, nq=nq, lam0=lam0)
    seq = lambda col0: pl.BlockSpec((None, S, HEAD_DIM), lambda h, b, *_: (b, 0, col0 + h))
    gs = pltpu.PrefetchScalarGridSpec(
        num_scalar_prefetch=2,
        grid=(N_HEADS, B),
        in_specs=[pl.BlockSpec(memory_space=pltpu.SMEM),
                  pl.BlockSpec((4, QK_DIM), lambda h, b, *_: (0, 0)),
                  seq(0), seq(N_HEADS), seq(2 * N_HEADS),
                  pl.BlockSpec((1, HEAD_DIM), lambda h, b, *_: (0, 0))],
        out_specs=seq(0),
        scratch_shapes=[pltpu.VMEM((nq, 2 * tq, 2 * HEAD_DIM), BF16), pltpu.VMEM((S, HEAD_DIM), BF16),
                        pltpu.VMEM((tq, tq), F32), pltpu.VMEM((nq, 2 * tq, LANES), F32),
                        pltpu.VMEM((nq, 2 * tq, 2 * HEAD_DIM), F32),
                        pltpu.VMEM((2 * tq, tq), F32), pltpu.VMEM((2 * tq, tq), F32)],
    )
    return pl.pallas_call(
        body,
        grid_spec=gs,
        out_shape=jax.ShapeDtypeStruct((B, S, ATTN_WIDTH), BF16),
        compiler_params=_cparams(("arbitrary", "arbitrary")),
        name="diffattn",
    )(qi_tab, j_tab, slopes, lamv, proj3, proj3, proj3, subln_g.reshape(1, HEAD_DIM))


def _conv_body(a_ref, g_ref, ah_ref, gh_ref, w_ref, cb_ref, lng_ref, lnb_ref, o_ref, u_ref, c_ref,
               *, ts, rb):
    i = pl.program_id(1)
    nchunk = u_ref.shape[0]
    u = a_ref[...].astype(F32) * jax.nn.sigmoid(g_ref[...].astype(F32))
    uh = ah_ref[...].astype(F32) * jax.nn.sigmoid(gh_ref[...].astype(F32))
    uh = jnp.where(i > 0, uh, 0.0)
    for c in range(nchunk):
        u_ref[c, pl.ds(0, CONV_HALO), :] = uh[:, c * LANES:(c + 1) * LANES]
        u_ref[c, pl.ds(CONV_HALO, ts), :] = u[:, c * LANES:(c + 1) * LANES]

    first = CONV_HALO - (CONV_TAPS - 1)

    def lane_chunk(c, carry):
        for r0 in range(0, ts, rb):
            acc = jnp.zeros((rb, LANES), F32)
            for k in range(CONV_TAPS):
                acc = acc + jnp.tile(w_ref[c, k], (rb // SUBLANES, 1)) * u_ref[c, pl.ds(r0 + first + k, rb), :]
            c_ref[c, pl.ds(r0, rb), :] = acc
        return carry

    lax.fori_loop(0, nchunk, lane_chunk, 0)

    C = nchunk * LANES
    cs = [c_ref[c] + cb_ref[:, c * LANES:(c + 1) * LANES] for c in range(nchunk)]
    mean = jnp.sum(functools.reduce(jnp.add, cs), axis=1, keepdims=True) * (1.0 / C)
    xcs = [x - mean for x in cs]
    var = jnp.sum(functools.reduce(jnp.add, [x * x for x in xcs]), axis=1, keepdims=True) * (1.0 / C)
    rstd = lax.rsqrt(var + EPS)
    for c in range(nchunk):
        sl = slice(c * LANES, (c + 1) * LANES)
        y = xcs[c] * rstd * lng_ref[:, sl] + lnb_ref[:, sl]
        o_ref[:, sl] = (y * jax.nn.sigmoid(y)).astype(o_ref.dtype)


def _conv(proj3, conv_w, conv_b, ln_g, ln_b, ts, rb):
    B, S, _ = proj3.shape
    C = conv_w.shape[-1]
    nchunk = C // LANES
    a_blk = (3 * ATTN_WIDTH) // C
    hpt = ts // CONV_HALO
    body = functools.partial(_conv_body, ts=ts, rb=rb)
    w_b = jnp.broadcast_to(conv_w.reshape(CONV_TAPS, nchunk, 1, LANES).transpose(1, 0, 2, 3),
                           (nchunk, CONV_TAPS, SUBLANES, LANES))
    halo = lambda blk: (lambda b, i: (b, jnp.maximum(i * hpt - 1, 0), blk))
    vec = pl.BlockSpec((1, C), lambda b, i: (0, 0))
    return pl.pallas_call(
        body,
        grid=(B, S // ts),
        in_specs=[pl.BlockSpec((None, ts, C), lambda b, i: (b, i, a_blk)),
                  pl.BlockSpec((None, ts, C), lambda b, i: (b, i, a_blk + 1)),
                  pl.BlockSpec((None, CONV_HALO, C), halo(a_blk)),
                  pl.BlockSpec((None, CONV_HALO, C), halo(a_blk + 1)),
                  pl.BlockSpec((nchunk, CONV_TAPS, SUBLANES, LANES), lambda b, i: (0, 0, 0, 0)),
                  vec, vec, vec],
        out_specs=pl.BlockSpec((None, ts, C), lambda b, i: (b, i, 0)),
        out_shape=jax.ShapeDtypeStruct((B, S, C), BF16),
        scratch_shapes=[pltpu.VMEM((nchunk, ts + CONV_HALO, LANES), F32), pltpu.VMEM((nchunk, ts, LANES), F32)],
        compiler_params=_cparams(("parallel", "arbitrary")),
        name="convmod",
    )(proj3, proj3, proj3, proj3, w_b, conv_b.reshape(1, C), ln_g.reshape(1, C), ln_b.reshape(1, C))


def _outproj_body(x_ref, a_ref, c_ref, wa_ref, wc_ref, g_ref, wr_ref, br_ref, h_ref, lg_ref, *, parts):
    rp = x_ref.shape[0] // parts
    for s in range(parts):
        rows = slice(s * rp, (s + 1) * rp)
        acc = (jnp.dot(a_ref[rows, :], wa_ref[...], preferred_element_type=F32)
               + jnp.dot(c_ref[rows, :], wc_ref[...], preferred_element_type=F32))
        h = x_ref[rows, :] + acc
        h_ref[rows, :] = h
        xn = _rmsnorm_rows(h, g_ref[...])
        hi = xn.astype(BF16)
        lo = (xn - hi.astype(F32)).astype(BF16)
        r = (jnp.dot(hi, wr_ref[...], preferred_element_type=F32)
             + jnp.dot(lo, wr_ref[...], preferred_element_type=F32))
        lg_ref[rows, :] = r[:, :ROUTER_COLS] + r[:, ROUTER_COLS:] + br_ref[...]


def _outproj(x2, attn2, conv2, w_attn, w_conv, g, wr, br, tm):
    T, D = x2.shape
    Ka, Kc = attn2.shape[1], conv2.shape[1]
    row = lambda n: pl.BlockSpec((tm, n), lambda i: (i, 0))
    full = lambda a, b: pl.BlockSpec((a, b), lambda i: (0, 0))
    return pl.pallas_call(
        functools.partial(_outproj_body, parts=4 if tm % 32 == 0 else 1),
        grid=(T // tm,),
        in_specs=[row(D), row(Ka), row(Kc), full(Ka, D), full(Kc, D), full(1, D),
                  full(D, 2 * ROUTER_COLS), full(1, ROUTER_COLS)],
        out_specs=[row(D), row(ROUTER_COLS)],
        out_shape=[jax.ShapeDtypeStruct((T, D), F32), jax.ShapeDtypeStruct((T, ROUTER_COLS), F32)],
        compiler_params=_cparams(("parallel",)),
        name="outproj",
    )(x2, attn2, conv2, w_attn, w_conv, g.reshape(1, D), wr, br)


def _route_body(lg_ref, id_ref, gw_ref, cnt_ref):
    lt = lg_ref[...].T
    tr = lt.shape[1]
    row = lax.broadcasted_iota(jnp.int32, (SUBLANES, tr), 0)
    gl = jnp.where(row < N_GROUPS, lt[0:SUBLANES], -jnp.inf)
    gmax = jnp.max(gl, axis=0, keepdims=True)
    g_sel = jnp.min(jnp.where(gl == gmax, row, SUBLANES), axis=0, keepdims=True)
    g_w = 1.0 / jnp.sum(jnp.exp(gl - gmax), axis=0, keepdims=True)
    e_in = lt[EXPERT_COL0:EXPERT_COL0 + EXPERTS_PER_GROUP]
    for g in range(1, N_GROUPS):
        lo = EXPERT_COL0 + g * EXPERTS_PER_GROUP
        e_in = jnp.where(g_sel == g, lt[lo:lo + EXPERTS_PER_GROUP], e_in)
    v1 = jnp.max(e_in, axis=0, keepdims=True)
    i1 = jnp.min(jnp.where(e_in == v1, row, SUBLANES), axis=0, keepdims=True)
    e_rest = jnp.where(row == i1, -jnp.inf, e_in)
    v2 = jnp.max(e_rest, axis=0, keepdims=True)
    i2 = jnp.min(jnp.where(e_rest == v2, row, SUBLANES), axis=0, keepdims=True)
    d = jnp.exp(v2 - v1)
    w1 = g_w / (1.0 + d)
    w2 = w1 * d
    e1 = g_sel * EXPERTS_PER_GROUP + i1
    e2 = g_sel * EXPERTS_PER_GROUP + i2
    id_ref[...] = jnp.where(row == 0, e1, jnp.where(row == 1, e2, 0))
    rowl = lax.broadcasted_iota(jnp.int32, (LANES, tr), 0)
    gw_ref[...] = jnp.where(rowl == 0, w1, jnp.where(rowl == 1, w2, 0.0)).T
    rowe = lax.broadcasted_iota(jnp.int32, (N_EXPERTS, tr), 0)
    hits = jnp.where(rowe == e1, 1.0, 0.0) + jnp.where(rowe == e2, 1.0, 0.0)

    @pl.when(pl.program_id(0) == 0)
    def _():
        cnt_ref[...] = jnp.zeros(cnt_ref.shape, F32)

    cnt_ref[...] += jnp.broadcast_to(jnp.sum(hits, axis=1, keepdims=True), cnt_ref.shape)


def _route(logits, tr):
    T = logits.shape[0]
    return pl.pallas_call(
        _route_body,
        grid=(T // tr,),
        in_specs=[pl.BlockSpec((tr, ROUTER_COLS), lambda i: (i, 0))],
        out_specs=[pl.BlockSpec((SUBLANES, tr), lambda i: (0, i)),
                   pl.BlockSpec((tr, LANES), lambda i: (i, 0)),
                   pl.BlockSpec((N_EXPERTS, LANES), lambda i: (0, 0))],
        out_shape=[jax.ShapeDtypeStruct((SUBLANES, T), jnp.int32),
                   jax.ShapeDtypeStruct((T, LANES), F32),
                   jax.ShapeDtypeStruct((N_EXPERTS, LANES), F32)],
        compiler_params=_cparams(("arbitrary",)),
        name="route",
    )(logits)


def _plan_body(id_ref, st_ref, u_ref, d_ref, base_ref):
    @pl.when(pl.program_id(0) == 0)
    def _():
        base_ref[...] = st_ref[...]

    ids = id_ref[...]
    tr = ids.shape[1]
    rowe = lax.broadcasted_iota(jnp.int32, (N_EXPERTS, tr), 0)
    base = base_ref[:, 0:1]
    dests = []
    for k in range(TOP_K):
        hit = rowe == ids[k:k + 1]
        before = jnp.dot(jnp.where(hit, 1.0, 0.0).astype(BF16), u_ref[...], preferred_element_type=F32)
        dests.append(jnp.sum(jnp.where(hit, base + before, 0.0), axis=0, keepdims=True))
        base = base + jnp.sum(jnp.where(hit, 1.0, 0.0), axis=1, keepdims=True)
    base_ref[...] = jnp.broadcast_to(base, base_ref.shape)
    row = lax.broadcasted_iota(jnp.int32, (SUBLANES, tr), 0)
    d_ref[...] = jnp.where(row == 0, dests[0], jnp.where(row == 1, dests[1], 0.0)).astype(jnp.int32)


def _plan(ids8, starts_rep, tr):
    T = ids8.shape[1]
    upper = jnp.asarray(np.triu(np.ones((tr, tr), np.float32), 1), BF16)
    return pl.pallas_call(
        _plan_body,
        grid=(T // tr,),
        in_specs=[pl.BlockSpec((SUBLANES, tr), lambda i: (0, i)),
                  pl.BlockSpec((N_EXPERTS, LANES), lambda i: (0, 0)),
                  pl.BlockSpec((tr, tr), lambda i: (0, 0))],
        out_specs=pl.BlockSpec((SUBLANES, tr), lambda i: (0, i)),
        out_shape=jax.ShapeDtypeStruct((SUBLANES, T), jnp.int32),
        scratch_shapes=[pltpu.VMEM((N_EXPERTS, LANES), F32)],
        compiler_params=_cparams(("arbitrary",)),
        name="plan",
    )(ids8, starts_rep, upper)


def _dispatch_body(endp_ref, npad_ref, d0_ref, d1_ref, h_ref, g_ref, xs_ref, xn_ref, zb_ref, sem, zsem,
                   *, tr, tm, n_steps):
    i = pl.program_id(0)
    slot = i % 2

    def wait_rows(s):
        for _ in range(TOP_K):
            pltpu.make_async_copy(xn_ref.at[s], xs_ref.at[pl.ds(0, tr), :], sem.at[s]).wait()

    @pl.when(i == 0)
    def _():
        zb_ref[...] = jnp.zeros(zb_ref.shape, F32)

        def zero_tail(e, carry):
            @pl.when(npad_ref[e] > 0)
            def _():
                tail = pl.multiple_of(endp_ref[e] - tm, tm)
                cp = pltpu.make_async_copy(zb_ref, xs_ref.at[pl.ds(tail, tm), :], zsem)
                cp.start()
                cp.wait()
            return carry

        lax.fori_loop(0, N_EXPERTS, zero_tail, 0)

        def zero_unused(j, carry):
            cp = pltpu.make_async_copy(zb_ref, xs_ref.at[pl.ds(pl.multiple_of(j * tm, tm), tm), :], zsem)
            cp.start()
            cp.wait()
            return carry

        lax.fori_loop(endp_ref[N_EXPERTS - 1] // tm, xs_ref.shape[0] // tm, zero_unused, 0)

    @pl.when(i >= 2)
    def _():
        wait_rows(slot)

    xn_ref[slot] = _rmsnorm_rows(h_ref[...], g_ref[...])

    def scatter(t, carry):
        src = xn_ref.at[slot, pl.ds(t, 1), :]
        pltpu.make_async_copy(src, xs_ref.at[pl.ds(d0_ref[t], 1), :], sem.at[slot]).start(priority=0)
        pltpu.make_async_copy(src, xs_ref.at[pl.ds(d1_ref[t], 1), :], sem.at[slot]).start(priority=1)
        return carry

    lax.fori_loop(0, tr, scatter, 0, unroll=8)

    @pl.when(i == n_steps - 1)
    def _():
        wait_rows(slot)
        if n_steps > 1:
            wait_rows(1 - slot)


def _dispatch(ends_p, n_pad, dest0, dest1, h1, g, n_rows, tr, tm):
    T, D = h1.shape
    n_steps = T // tr
    gs = pltpu.PrefetchScalarGridSpec(
        num_scalar_prefetch=2,
        grid=(n_steps,),
        in_specs=[pl.BlockSpec((tr,), lambda i, ep, npd: (i,), memory_space=pltpu.SMEM),
                  pl.BlockSpec((tr,), lambda i, ep, npd: (i,), memory_space=pltpu.SMEM),
                  pl.BlockSpec((tr, D), lambda i, ep, npd: (i, 0)),
                  pl.BlockSpec((1, D), lambda i, ep, npd: (0, 0))],
        out_specs=pl.BlockSpec(memory_space=pl.ANY),
        scratch_shapes=[pltpu.VMEM((2, tr, D), F32), pltpu.VMEM((tm, D), F32),
                        pltpu.SemaphoreType.DMA((2,)), pltpu.SemaphoreType.DMA(())],
    )
    return pl.pallas_call(
        functools.partial(_dispatch_body, tr=tr, tm=tm, n_steps=n_steps),
        grid_spec=gs,
        out_shape=jax.ShapeDtypeStruct((n_rows, D), F32),
        compiler_params=_cparams(("arbitrary",)),
        name="dispatch",
    )(ends_p, n_pad, dest0, dest1, h1, g.reshape(1, D))


def _moe_body(te_ref, nu_ref, ne_ref, cb_ref, ce_ref, sl_ref,
              x_ref, wg_hbm, wu_hbm, wd_hbm, y_ref,
              wg_s, wu_s, wd_s, sg, su, sd, sem):
    i = pl.program_id(0)
    slot = sl_ref[i]
    fu = wg_s.shape[-1] // W_UNITS

    def unit_copies(e, c, par):
        col = pl.multiple_of(c * fu, fu)
        return (pltpu.make_async_copy(wg_hbm.at[e, :, pl.ds(col, fu)], sg.at[par], sem.at[par]),
                pltpu.make_async_copy(wu_hbm.at[e, :, pl.ds(col, fu)], su.at[par], sem.at[par]),
                pltpu.make_async_copy(wd_hbm.at[e, pl.ds(col, fu), :], sd.at[par], sem.at[par]))

    def start_unit(e, c, par):
        for cp in unit_copies(e, c, par):
            cp.start()

    def wait_unit(e, c, par):
        for cp in unit_copies(e, c, par):
            cp.wait()

    def convert_unit(c, par, dst_slot):
        col = pl.multiple_of(c * fu, fu)
        wg_s[dst_slot, :, pl.ds(col, fu)] = sg[par].astype(BF16)
        wu_s[dst_slot, :, pl.ds(col, fu)] = su[par].astype(BF16)
        wd_s[dst_slot, pl.ds(col, fu), :] = sd[par].astype(BF16)

    def load_units(e, c_lo, c_hi, dst_slot):
        def unit(c, carry):
            par = c % 2
            wait_unit(e, c, par)

            @pl.when(c + 1 < W_UNITS)
            def _():
                start_unit(e, c + 1, 1 - par)

            convert_unit(c, par, dst_slot)
            return carry

        lax.fori_loop(c_lo, c_hi, unit, 0)

    @pl.when(i == 0)
    def _():
        start_unit(te_ref[0], 0, 0)
        load_units(te_ref[0], 0, W_UNITS, slot)

    @pl.when(jnp.logical_and(cb_ref[i] == 0, ce_ref[i] > 0))
    def _():
        start_unit(ne_ref[i], 0, 0)

    @pl.when(i < nu_ref[0])
    def _():
        x = x_ref[...].astype(BF16)
        a = jnp.dot(x, wg_s[slot], preferred_element_type=F32)
        b = jnp.dot(x, wu_s[slot], preferred_element_type=F32)
        hid = (a * jax.nn.sigmoid(a) * b).astype(BF16)
        y_ref[...] = jnp.dot(hid, wd_s[slot], preferred_element_type=F32)

    @pl.when(i >= nu_ref[0])
    def _():
        y_ref[...] = jnp.zeros(y_ref.shape, y_ref.dtype)

    load_units(ne_ref[i], cb_ref[i], ce_ref[i], 1 - slot)


def _moe(plan, x_sorted, w_gate, w_up, w_down, tm):
    P, D = x_sorted.shape
    F = w_gate.shape[-1]
    fu = F // W_UNITS
    n_pref = len(plan)
    x_map = lambda i, te, nu, *_: (jnp.minimum(i, nu[0] - 1), 0)
    hbm = pl.BlockSpec(memory_space=pl.ANY)
    gs = pltpu.PrefetchScalarGridSpec(
        num_scalar_prefetch=n_pref,
        grid=(P // tm,),
        in_specs=[pl.BlockSpec((tm, D), x_map), hbm, hbm, hbm],
        out_specs=pl.BlockSpec((tm, D), lambda i, *_: (i, 0)),
        scratch_shapes=[pltpu.VMEM((2, D, F), BF16), pltpu.VMEM((2, D, F), BF16), pltpu.VMEM((2, F, D), BF16),
                        pltpu.VMEM((2, D, fu), F32), pltpu.VMEM((2, D, fu), F32), pltpu.VMEM((2, fu, D), F32),
                        pltpu.SemaphoreType.DMA((2,))],
    )
    return pl.pallas_call(
        _moe_body,
        grid_spec=gs,
        out_shape=jax.ShapeDtypeStruct((P, D), F32),
        compiler_params=_cparams(("arbitrary",)),
        name="moe",
    )(*plan, x_sorted, w_gate, w_up, w_down)


def _combine_body(d0_ref, d1_ref, h_ref, gw_ref, g_ref, ys_hbm, o_ref, yb_ref, sem, *, tr, th, final_norm):
    halves = tr // th

    def gather(hf):
        def one(t, carry):
            tok = hf * th + t
            pltpu.make_async_copy(ys_hbm.at[pl.ds(d0_ref[tok], 1), :], yb_ref.at[hf, 0, pl.ds(t, 1), :],
                                  sem.at[hf]).start(priority=0)
            pltpu.make_async_copy(ys_hbm.at[pl.ds(d1_ref[tok], 1), :], yb_ref.at[hf, 1, pl.ds(t, 1), :],
                                  sem.at[hf]).start(priority=1)
            return carry

        lax.fori_loop(0, th, one, 0, unroll=8)

    for hf in range(halves):
        gather(hf)
    for hf in range(halves):
        for k in range(TOP_K):
            pltpu.make_async_copy(ys_hbm.at[pl.ds(0, th), :], yb_ref.at[hf, k], sem.at[hf]).wait()
        rows = pl.ds(hf * th, th)
        gw = gw_ref[rows, :]
        h = h_ref[rows, :] + gw[:, 0:1] * yb_ref[hf, 0] + gw[:, 1:2] * yb_ref[hf, 1]
        if final_norm:
            h = _rmsnorm_rows(h, g_ref[...])
        o_ref[rows, :] = h


def _combine(dest0, dest1, h1, gwt, g, y_sorted, final_norm, tr, th):
    T, D = h1.shape
    smem = pl.BlockSpec((tr,), lambda i: (i,), memory_space=pltpu.SMEM)
    row = pl.BlockSpec((tr, D), lambda i: (i, 0))
    return pl.pallas_call(
        functools.partial(_combine_body, tr=tr, th=th, final_norm=final_norm),
        grid=(T // tr,),
        in_specs=[smem, smem, row, pl.BlockSpec((tr, LANES), lambda i: (i, 0)),
                  pl.BlockSpec((1, D), lambda i: (0, 0)), pl.BlockSpec(memory_space=pl.ANY)],
        out_specs=row,
        out_shape=jax.ShapeDtypeStruct((T, D), F32),
        scratch_shapes=[pltpu.VMEM((tr // th, TOP_K, th, D), F32), pltpu.SemaphoreType.DMA((tr // th,))],
        compiler_params=_cparams(("arbitrary",)),
        name="combine",
    )(dest0, dest1, h1, gwt, g.reshape(1, D), y_sorted)


def _router_weights(w_group, b_group, w_expert, b_expert):
    def pack(g, e):
        gap = jnp.zeros(g.shape[:-1] + (EXPERT_COL0 - N_GROUPS,), F32)
        tail = jnp.zeros(g.shape[:-1] + (ROUTER_COLS - EXPERT_COL0 - N_EXPERTS,), F32)
        return jnp.concatenate([g.astype(F32), gap, e.astype(F32), tail], axis=-1)

    w = pack(w_group, w_expert)
    b = pack(b_group[None, :], b_expert[None, :])
    w_hi = w.astype(BF16)
    w_lo = (w - w_hi.astype(F32)).astype(BF16)
    return jnp.concatenate([w_hi, w_lo], axis=1), b


def _tile_plan(counts, tm, n_tiles):
    i32 = jnp.int32
    eidx = jnp.arange(N_EXPERTS, dtype=i32)
    tiles_e = (counts + tm - 1) // tm
    n_pad = tiles_e * tm
    ends_p = jnp.cumsum(n_pad).astype(i32)
    starts_p = ends_p - n_pad
    cum_tiles = jnp.cumsum(tiles_e).astype(i32)
    n_used = cum_tiles[-1]
    used = tiles_e > 0
    last_e = jnp.max(jnp.where(used, eidx, 0))
    ti = jnp.arange(n_tiles, dtype=i32)
    te = jnp.minimum(jnp.sum((ti[:, None] >= cum_tiles[None, :]).astype(i32), axis=1), last_e)
    is_te = te[:, None] == eidx[None, :]
    pick = lambda v: jnp.sum(jnp.where(is_te, v[None, :], 0), axis=1)
    t_in = ti - pick(cum_tiles - tiles_e)
    t_n = jnp.maximum(pick(tiles_e), 1)
    nxt = jnp.min(jnp.where(used[None, :] & (eidx[None, :] > eidx[:, None]), eidx[None, :], N_EXPERTS), axis=1)
    nxt_te = pick(nxt)
    has_next = (nxt_te < N_EXPERTS) & (ti < n_used)
    ne = jnp.where(has_next, nxt_te, 0).astype(i32)
    cb = jnp.where(has_next, (W_UNITS * t_in) // t_n, 0).astype(i32)
    ce = jnp.where(has_next, (W_UNITS * (t_in + 1)) // t_n, 0).astype(i32)
    slot = (pick(jnp.cumsum(used.astype(i32)) - 1) % 2).astype(i32)
    moe_plan = (te.astype(i32), n_used.reshape(1).astype(i32), ne, cb, ce, slot)
    return starts_p, ends_p, n_pad.astype(i32), moe_plan


def kernel(x, norm_mix_g, w_in, lambda_q1, lambda_k1, lambda_q2, lambda_k2, subln_g, conv_w, conv_b,
           conv_ln_g, conv_ln_b, w_out, norm_ffn_g, w_group_router, b_group_router, w_expert_router,
           b_expert_router, w_gate, w_up, w_down, norm_final_g):
    B, S, D = x.shape
    T = B * S
    depth = w_in.shape[0]
    tm_proj = min(512, T)
    tm_out = min(512, T)
    tq = min(512, S)
    ts = min(512, S)
    tm_moe = 256
    tr = min(1024, T)
    slopes = 2.0 ** (-8.0 * (jnp.arange(N_HEADS, dtype=F32) + 1.0) / N_HEADS)

    h = x.reshape(T, D)
    for l in range(depth):
        lam0 = 0.8 - 0.6 * math.exp(-0.3 * l)
        lamv = jnp.stack([lambda_q1[l], lambda_k1[l], lambda_q2[l], lambda_k2[l]]).astype(F32)
        proj = _inproj(h, norm_mix_g[l], w_in[l].astype(BF16), tm_proj)
        proj3 = proj.reshape(B, S, proj.shape[1])
        attn = _attention(proj3, slopes, lamv, subln_g[l], lam0, tq)
        conv = _conv(proj3, conv_w[l], conv_b[l], conv_ln_g[l], conv_ln_b[l], ts, 64)
        w_o = w_out[l].astype(BF16)
        wr, br = _router_weights(w_group_router[l], b_group_router[l], w_expert_router[l], b_expert_router[l])
        h1, logits = _outproj(h, attn.reshape(T, ATTN_WIDTH), conv.reshape(T, -1),
                              w_o[:ATTN_WIDTH], w_o[ATTN_WIDTH:], norm_ffn_g[l], wr, br, tm_out)
        ids8, gwt, cnt = _route(logits, tr)
        n_rows = T * TOP_K + N_EXPERTS * tm_moe
        starts_p, ends_p, n_pad, moe_plan = _tile_plan(cnt[:, 0].astype(jnp.int32), tm_moe, n_rows // tm_moe)
        starts_rep = jnp.broadcast_to(starts_p.astype(F32)[:, None], (N_EXPERTS, LANES))
        dest8 = _plan(ids8, starts_rep, tr)
        dest0, dest1 = dest8[0], dest8[1]
        x_sorted = _dispatch(ends_p, n_pad, dest0, dest1, h1, norm_ffn_g[l], n_rows, tr, tm_moe)
        y_sorted = _moe(moe_plan, x_sorted, w_gate[l], w_up[l], w_down[l], tm_moe)
        h = _combine(dest0, dest1, h1, gwt, norm_final_g, y_sorted, l == depth - 1, tr, tr // 2)
    return h.reshape(B, S, D)
```
